```python
import jax
import jax.numpy as jnp
from jax import lax
import numpy as np

D_MODEL = 2048
BATCH = 2
SEQ = 4096
DEPTH = 2
DEC_BATCH = 128
DEC_SEQ = 8
PAST_LEN = 16384
PAGE_SIZE = 128

N_EVEN = (DEPTH + 1) // 2
N_ODD = DEPTH // 2

FOX_HEADS = 8
FOX_KV_HEADS = 2
FOX_HEAD_DIM = 128
FOX_WIDTH = FOX_HEADS * FOX_HEAD_DIM
FOX_KV_WIDTH = FOX_KV_HEADS * FOX_HEAD_DIM
FOX_BIAS_INIT = 4.0

HG_HEADS = 8
HG_HEAD_DIM = 128
HG_WIDTH = HG_HEADS * HG_HEAD_DIM
HG_CHUNK = 64

EVEN_SIZES = (FOX_WIDTH, FOX_KV_WIDTH, FOX_KV_WIDTH, FOX_HEADS, HG_WIDTH, HG_WIDTH, HG_WIDTH, HG_WIDTH)
EVEN_IN = FOX_WIDTH + 2 * FOX_KV_WIDTH + FOX_HEADS + 4 * HG_WIDTH

MLA_HEADS = 16
MLA_NOPE = 128
MLA_ROPE = 64
MLA_V = 128
MLA_Q_RANK = 512
MLA_KV_RANK = 512
ROPE_THETA = 10000.0
ODD_SIZES = (MLA_Q_RANK, MLA_KV_RANK, MLA_ROPE)
ODD_IN = MLA_Q_RANK + MLA_KV_RANK + MLA_ROPE

MEM_LEN = 256
MEM_HEADS = 4
MEM_HEAD_DIM = 128
MEM_WIDTH = MEM_HEADS * MEM_HEAD_DIM

D_FF = 4 * D_MODEL
Q_BLOCK = 128
EPS = 1e-6
F32 = jnp.float32

kernel_name = 'fox_hgrn2_mla_memory_decoder_step'


def _rmsnorm(x, g):
    xf = x.astype(F32)
    y = xf * lax.rsqrt(jnp.mean(xf * xf, axis=-1, keepdims=True) + EPS)
    return (y * g.astype(F32)).astype(x.dtype)


def _split(z, sizes):
    parts, start = [], 0
    for n in sizes:
        parts.append(z[..., start:start + n])
        start += n
    return parts


def _rope(x, pos):
    half = x.shape[-1] // 2
    inv_freq = ROPE_THETA ** (-jnp.arange(half, dtype=F32) / half)
    ang = pos.astype(F32)[:, None] * inv_freq[None, :]
    cos = jnp.cos(ang)[:, None, :]
    sin = jnp.sin(ang)[:, None, :]
    x1 = x[..., :half].astype(F32)
    x2 = x[..., half:].astype(F32)
    return jnp.concatenate([x1 * cos - x2 * sin, x2 * cos + x1 * sin], axis=-1).astype(x.dtype)


def _blocked_causal_attention(q, k, v, cum):
    B, S, H, dq = q.shape
    G = k.shape[2]
    R = H // G
    dv = v.shape[-1]
    scale = dq ** -0.5
    key_pos = jnp.arange(S)
    cum_t = None if cum is None else cum.reshape(B, S, G, R).transpose(0, 2, 3, 1)

    def one_block(i):
        start = i * Q_BLOCK
        qi = lax.dynamic_slice_in_dim(q, start, Q_BLOCK, axis=1).reshape(B, Q_BLOCK, G, R, dq)
        s = jnp.einsum('bqgrd,bkgd->bgrqk', qi, k).astype(F32) * scale
        if cum_t is not None:
            ci = lax.dynamic_slice_in_dim(cum_t, start, Q_BLOCK, axis=3)
            s = s + ci[..., :, None] - cum_t[..., None, :]
        qpos = start + jnp.arange(Q_BLOCK)
        s = jnp.where(key_pos[None, :] <= qpos[:, None], s, -jnp.inf)
        p = jax.nn.softmax(s, axis=-1).astype(v.dtype)
        o = jnp.einsum('bgrqk,bkgd->bqgrd', p, v)
        return o.reshape(B, Q_BLOCK, H * dv)

    out = lax.map(one_block, jnp.arange(S // Q_BLOCK))
    return out.transpose(1, 0, 2, 3).reshape(B, S, H * dv)


def _hgrn_lower_bounds(gamma):
    return jnp.cumsum(jax.nn.softmax(gamma.astype(F32), axis=0), axis=0)[:-1]


def _hgrn2(q, k, v, logf, s0):
    B, T, H, dk = q.shape
    dv = v.shape[-1]
    c = min(HG_CHUNK, T)
    n = -(-T // c)
    pad = n * c - T
    if pad:
        pw = ((0, 0), (0, pad), (0, 0), (0, 0))
        q, k, v, logf = (jnp.pad(a, pw) for a in (q, k, v, logf))

    def to_chunks(a):
        return a.reshape(B, n, c, H, a.shape[-1]).transpose(1, 0, 3, 2, 4)

    causal = jnp.tril(jnp.ones((c, c), dtype=bool))

    def step(S, xs):
        qc, kc, vc, lc = xs
        b = jnp.cumsum(lc, axis=2)
        b_last = b[:, :, -1:, :]
        q_e = qc * jnp.exp(b)
        k_e = kc * jnp.exp(-b)
        att = jnp.where(causal, jnp.einsum('bhtk,bhsk->bhts', q_e, k_e), 0.0)
        o = jnp.einsum('bhts,bhsv->bhtv', att, vc) + jnp.einsum('bhtk,bhkv->bhtv', q_e, S)
        S_new = (jnp.exp(b_last[:, :, 0, :])[..., None] * S
                 + jnp.einsum('bhsk,bhsv->bhkv', kc * jnp.exp(b_last - b), vc))
        return S_new, o

    S, o = lax.scan(step, s0, (to_chunks(q), to_chunks(k), to_chunks(v), to_chunks(logf)))
    o = o.transpose(1, 0, 3, 2, 4).reshape(B, n * c, H, dv)[:, :T]
    return o, S


def _even_project(h, e, ev_w_in, ev_fox_bf, ev_hg_gamma):
    B, T, _ = h.shape
    fq, fk, fv, fz, hq, hf, hi, hg = _split(h @ ev_w_in[e], EVEN_SIZES)
    fox = (fq.reshape(B, T, FOX_HEADS, FOX_HEAD_DIM),
           fk.reshape(B, T, FOX_KV_HEADS, FOX_HEAD_DIM),
           fv.reshape(B, T, FOX_KV_HEADS, FOX_HEAD_DIM),
           jax.nn.log_sigmoid((fz + ev_fox_bf[e]).astype(F32)))
    lb = _hgrn_lower_bounds(ev_hg_gamma)[e]
    f = lb + (1.0 - lb) * jax.nn.sigmoid(hf.astype(F32))

    def hs(a):
        return a.reshape(B, T, HG_HEADS, HG_HEAD_DIM)

    hgrn = (hs(hq.astype(F32)), hs(1.0 - f), hs(hi.astype(F32)), hs(jnp.log(f)), hs(hg))
    return fox, hgrn


def _even_output(fox_o, hg_o, hg_gate, hg_norm, w_out):
    B, T = fox_o.shape[:2]
    g = _rmsnorm(hg_o, hg_norm) * jax.nn.silu(hg_gate.astype(F32))
    cat = jnp.concatenate([fox_o, g.reshape(B, T, HG_WIDTH).astype(fox_o.dtype)], axis=-1)
    return cat @ w_out


def _fox_decode(q, k_new, v_new, lf_new, pool_k, pool_v, pool_lf, e, page_table):
    Bd, T, H, d = q.shape
    G = k_new.shape[2]
    R = H // G
    scale = d ** -0.5
    qg = q.reshape(Bd, T, G, R, d)
    cnew = jnp.cumsum(lf_new, axis=1).reshape(Bd, T, G, R).transpose(0, 2, 3, 1)
    causal = jnp.tril(jnp.ones((T, T), dtype=bool))
    s = (jnp.einsum('btgrd,bsgd->bgrts', qg, k_new).astype(F32) * scale
         + cnew[..., :, None] - cnew[..., None, :])
    s = jnp.where(causal, s, -jnp.inf)
    m = jnp.max(s, axis=-1)
    p = jnp.exp(s - m[..., None])
    l = jnp.sum(p, axis=-1)
    acc = jnp.einsum('bgrts,bsgd->bgrtd', p, v_new.astype(F32))

    def page_step(carry, pt):
        m, l, acc, suf = carry
        kp = pool_k[e, pt]
        vp = pool_v[e, pt]
        cw = jnp.cumsum(pool_lf[e, pt].astype(F32), axis=1)
        cw = cw.reshape(Bd, -1, G, R).transpose(0, 2, 3, 1)
        tot = cw[..., -1]
        row_bias = suf[..., None] + tot[..., None] - cw
        sp = (jnp.einsum('btgrd,bsgd->bgrts', qg, kp).astype(F32) * scale
              + cnew[..., :, None] + row_bias[..., None, :])
        m_new = jnp.maximum(m, jnp.max(sp, axis=-1))
        alpha = jnp.exp(m - m_new)
        pp = jnp.exp(sp - m_new[..., None])
        l = l * alpha + jnp.sum(pp, axis=-1)
        acc = acc * alpha[..., None] + jnp.einsum('bgrts,bsgd->bgrtd', pp, vp.astype(F32))
        return (m_new, l, acc, suf + tot), None

    init = (m, l, acc, jnp.zeros((Bd, G, R), F32))
    (m, l, acc, _), _ = lax.scan(page_step, init, page_table.T, reverse=True)
    out = acc / l[..., None]
    return out.transpose(0, 3, 1, 2, 4).reshape(Bd, T, H * d).astype(v_new.dtype)


def _mla_project(h, pos, w_in, q_norm, w_qb, kv_norm):
    B, T, _ = h.shape
    qa, ckv, kpe = _split(h @ w_in, ODD_SIZES)
    q = (_rmsnorm(qa, q_norm) @ w_qb).reshape(B, T, MLA_HEADS, MLA_NOPE + MLA_ROPE)
    q_nope = q[..., :MLA_NOPE]
    q_pe = _rope(q[..., MLA_NOPE:], pos)
    ckv = _rmsnorm(ckv, kv_norm)
    kpe = _rope(kpe[:, :, None, :], pos)[:, :, 0, :]
    return q_nope, q_pe, ckv, kpe


def _mla_prompt(q_nope, q_pe, ckv, kpe, w_uk, w_uv):
    B, T, H, _ = q_nope.shape
    k_nope = jnp.einsum('btc,chn->bthn', ckv, w_uk)
    v = jnp.einsum('btc,chv->bthv', ckv, w_uv)
    q = jnp.concatenate([q_nope, q_pe], axis=-1)
    k = jnp.concatenate([k_nope, jnp.broadcast_to(kpe[:, :, None, :], (B, T, H, MLA_ROPE))], axis=-1)
    return _blocked_causal_attention(q, k, v, None)


def _mla_decode(q_nope, q_pe, ckv_new, kpe_new, pool_ckv, pool_kpe, o, page_table, w_uk, w_uv):
    Bd, T, H, _ = q_nope.shape
    scale = (MLA_NOPE + MLA_ROPE) ** -0.5
    q_lat = jnp.einsum('bthn,chn->bthc', q_nope, w_uk)

    def scores(ckv, kpe):
        return (jnp.einsum('bthc,bsc->bhts', q_lat, ckv)
                + jnp.einsum('bthr,bsr->bhts', q_pe, kpe)).astype(F32) * scale

    causal = jnp.tril(jnp.ones((T, T), dtype=bool))
    s = jnp.where(causal, scores(ckv_new, kpe_new), -jnp.inf)
    m = jnp.max(s, axis=-1)
    p = jnp.exp(s - m[..., None])
    l = jnp.sum(p, axis=-1)
    acc = jnp.einsum('bhts,bsc->bhtc', p, ckv_new.astype(F32))

    def page_step(carry, pt):
        m, l, acc = carry
        ckv = pool_ckv[o, pt]
        sp = scores(ckv, pool_kpe[o, pt])
        m_new = jnp.maximum(m, jnp.max(sp, axis=-1))
        alpha = jnp.exp(m - m_new)
        pp = jnp.exp(sp - m_new[..., None])
        l = l * alpha + jnp.sum(pp, axis=-1)
        acc = acc * alpha[..., None] + jnp.einsum('bhts,bsc->bhtc', pp, ckv.astype(F32))
        return (m_new, l, acc), None

    (m, l, acc), _ = lax.scan(page_step, (m, l, acc), page_table.T)
    out_lat = (acc / l[..., None]).astype(w_uv.dtype)
    return jnp.einsum('bhtc,chv->bthv', out_lat, w_uv).reshape(Bd, T, H * MLA_V)


def _mem_kv(mem, g, wk, wv):
    B, M, _ = mem.shape
    mn = _rmsnorm(mem, g)
    return ((mn @ wk).reshape(B, M, MEM_HEADS, MEM_HEAD_DIM),
            (mn @ wv).reshape(B, M, MEM_HEADS, MEM_HEAD_DIM))


def _mem_attend(h, wq, wo, mk, mv):
    B, T, _ = h.shape
    q = (h @ wq).reshape(B, T, MEM_HEADS, MEM_HEAD_DIM)
    s = jnp.einsum('bthd,bmhd->bhtm', q, mk).astype(F32) * MEM_HEAD_DIM ** -0.5
    p = jax.nn.softmax(s, axis=-1).astype(mv.dtype)
    out = jnp.einsum('bhtm,bmhd->bthd', p, mv).reshape(B, T, MEM_WIDTH)
    return out @ wo


def _sqrelu_mlp(h, w1, w2):
    a = jax.nn.relu(h @ w1)
    return (a * a) @ w2


def setup_inputs(seed: int = 0) -> dict:
    key = jax.random.key(seed)
    ks = iter(jax.random.split(key, 48))

    def nrm(shape, scale=1.0):
        return jax.random.normal(next(ks), shape, F32) * scale

    def gain(shape):
        return 1.0 + 0.02 * jax.random.normal(next(ks), shape, F32)

    D = D_MODEL
    n_pages = PAST_LEN // PAGE_SIZE
    n_used = DEC_BATCH * n_pages
    n_phys = (n_used * 5) // 4
    return {
        'x_prompt': nrm((BATCH, SEQ, D)),
        'x_sample': nrm((DEC_BATCH, DEC_SEQ, D)),
        'mem_prompt': nrm((BATCH, MEM_LEN, D)),
        'page_table': jax.random.permutation(next(ks), n_phys)[:n_used].reshape(DEC_BATCH, n_pages).astype(jnp.int32),
        'cache_fox_k': nrm((N_EVEN, n_phys, PAGE_SIZE, FOX_KV_HEADS, FOX_HEAD_DIM)),
        'cache_fox_v': nrm((N_EVEN, n_phys, PAGE_SIZE, FOX_KV_HEADS, FOX_HEAD_DIM)),
        'cache_fox_lf': jax.nn.log_sigmoid(FOX_BIAS_INIT + nrm((N_EVEN, n_phys, PAGE_SIZE, FOX_HEADS))),
        'state_hgrn': nrm((N_EVEN, DEC_BATCH, HG_HEADS, HG_HEAD_DIM, HG_HEAD_DIM), 0.5),
        'cache_mla_ckv': nrm((N_ODD, n_phys, PAGE_SIZE, MLA_KV_RANK)),
        'cache_mla_kpe': nrm((N_ODD, n_phys, PAGE_SIZE, MLA_ROPE)),
        'cache_mem_k': nrm((DEPTH, DEC_BATCH, MEM_LEN, MEM_HEADS, MEM_HEAD_DIM)),
        'cache_mem_v': nrm((DEPTH, DEC_BATCH, MEM_LEN, MEM_HEADS, MEM_HEAD_DIM)),
        'ev_norm': gain((N_EVEN, D)),
        'ev_w_in': nrm((N_EVEN, D, EVEN_IN), D ** -0.5),
        'ev_fox_bf': FOX_BIAS_INIT + nrm((N_EVEN, FOX_HEADS), 0.1),
        'ev_hg_gamma': nrm((N_EVEN + 1, HG_WIDTH), 0.1),
        'ev_hg_norm': gain((N_EVEN, HG_HEAD_DIM)),
        'ev_w_out': nrm((N_EVEN, FOX_WIDTH + HG_WIDTH, D), (FOX_WIDTH + HG_WIDTH) ** -0.5),
        'od_norm': gain((N_ODD, D)),
        'od_w_in': nrm((N_ODD, D, ODD_IN), D ** -0.5),
        'od_q_norm': gain((N_ODD, MLA_Q_RANK)),
        'od_w_qb': nrm((N_ODD, MLA_Q_RANK, MLA_HEADS * (MLA_NOPE + MLA_ROPE)), MLA_Q_RANK ** -0.5),
        'od_kv_norm': gain((N_ODD, MLA_KV_RANK)),
        'od_w_uk': nrm((N_ODD, MLA_KV_RANK, MLA_HEADS, MLA_NOPE), MLA_KV_RANK ** -0.5),
        'od_w_uv': nrm((N_ODD, MLA_KV_RANK, MLA_HEADS, MLA_V), MLA_KV_RANK ** -0.5),
        'od_w_out': nrm((N_ODD, MLA_HEADS * MLA_V, D), (MLA_HEADS * MLA_V) ** -0.5),
        'xa_norm': gain((DEPTH, D)),
        'xa_mem_norm': gain((DEPTH, D)),
        'xa_wq': nrm((DEPTH, D, MEM_WIDTH), D ** -0.5),
        'xa_wk': nrm((DEPTH, D, MEM_WIDTH), D ** -0.5),
        'xa_wv': nrm((DEPTH, D, MEM_WIDTH), D ** -0.5),
        'xa_wo': nrm((DEPTH, MEM_WIDTH, D), MEM_WIDTH ** -0.5),
        'ff_norm': gain((DEPTH, D)),
        'ff_w1': nrm((DEPTH, D, D_FF), D ** -0.5),
        'ff_w2': nrm((DEPTH, D_FF, D), D_FF ** -0.5),
        'final_norm': gain((D,)),
    }


def reference(x_prompt, x_sample, mem_prompt, page_table, cache_fox_k, cache_fox_v, cache_fox_lf, state_hgrn,
              cache_mla_ckv, cache_mla_kpe, cache_mem_k, cache_mem_v, ev_norm, ev_w_in, ev_fox_bf, ev_hg_gamma,
              ev_hg_norm, ev_w_out, od_norm, od_w_in, od_q_norm, od_w_qb, od_kv_norm, od_w_uk, od_w_uv, od_w_out,
              xa_norm, xa_mem_norm, xa_wq, xa_wk, xa_wv, xa_wo, ff_norm, ff_w1, ff_w2, final_norm):
    xp, xs = x_prompt, x_sample
    Bp = xp.shape[0]
    pos_p = jnp.arange(xp.shape[1])
    pos_s = PAST_LEN + jnp.arange(xs.shape[1])
    p_fk, p_fv, p_flf, p_hs, p_ckv, p_kpe, p_mk, p_mv = [], [], [], [], [], [], [], []
    s_fk, s_fv, s_flf, s_hs, s_ckv, s_kpe = [], [], [], [], [], []
    for layer in range(DEPTH):
        if layer % 2 == 0:
            e = layer // 2
            (fq, fk, fv, flf), (hq, hk, hv, hlf, hg) = _even_project(
                _rmsnorm(xp, ev_norm[e]), e, ev_w_in, ev_fox_bf, ev_hg_gamma)
            fo = _blocked_causal_attention(fq, fk, fv, jnp.cumsum(flf, axis=1))
            ho, hS = _hgrn2(hq, hk, hv, hlf, jnp.zeros((Bp, HG_HEADS, HG_HEAD_DIM, HG_HEAD_DIM), F32))
            xp = xp + _even_output(fo, ho, hg, ev_hg_norm[e], ev_w_out[e])
            p_fk.append(fk)
            p_fv.append(fv)
            p_flf.append(flf)
            p_hs.append(hS)
            (fq, fk, fv, flf), (hq, hk, hv, hlf, hg) = _even_project(
                _rmsnorm(xs, ev_norm[e]), e, ev_w_in, ev_fox_bf, ev_hg_gamma)
            fo = _fox_decode(fq, fk, fv, flf, cache_fox_k, cache_fox_v, cache_fox_lf, e, page_table)
            ho, hS = _hgrn2(hq, hk, hv, hlf, state_hgrn[e].astype(F32))
            xs = xs + _even_output(fo, ho, hg, ev_hg_norm[e], ev_w_out[e])
            s_fk.append(fk)
            s_fv.append(fv)
            s_flf.append(flf)
            s_hs.append(hS)
        else:
            o = layer // 2
            qn, qr, ckv, kpe = _mla_project(_rmsnorm(xp, od_norm[o]), pos_p, od_w_in[o], od_q_norm[o],
                                            od_w_qb[o], od_kv_norm[o])
            xp = xp + _mla_prompt(qn, qr, ckv, kpe, od_w_uk[o], od_w_uv[o]) @ od_w_out[o]
            p_ckv.append(ckv)
            p_kpe.append(kpe)
            qn, qr, ckv, kpe = _mla_project(_rmsnorm(xs, od_norm[o]), pos_s, od_w_in[o], od_q_norm[o],
                                            od_w_qb[o], od_kv_norm[o])
            xs = xs + _mla_decode(qn, qr, ckv, kpe, cache_mla_ckv, cache_mla_kpe, o, page_table,
                                  od_w_uk[o], od_w_uv[o]) @ od_w_out[o]
            s_ckv.append(ckv)
            s_kpe.append(kpe)
        mk, mv = _mem_kv(mem_prompt, xa_mem_norm[layer], xa_wk[layer], xa_wv[layer])
        p_mk.append(mk)
        p_mv.append(mv)
        xp = xp + _mem_attend(_rmsnorm(xp, xa_norm[layer]), xa_wq[layer], xa_wo[layer], mk, mv)
        xs = xs + _mem_attend(_rmsnorm(xs, xa_norm[layer]), xa_wq[layer], xa_wo[layer],
                              cache_mem_k[layer], cache_mem_v[layer])
        xp = xp + _sqrelu_mlp(_rmsnorm(xp, ff_norm[layer]), ff_w1[layer], ff_w2[layer])
        xs = xs + _sqrelu_mlp(_rmsnorm(xs, ff_norm[layer]), ff_w1[layer], ff_w2[layer])
    y_prompt = _rmsnorm(xp, final_norm)
    y_sample = _rmsnorm(xs, final_norm)
    return (y_prompt, y_sample,
            jnp.stack(p_fk), jnp.stack(p_fv), jnp.stack(p_flf), jnp.stack(p_hs),
            jnp.stack(p_ckv), jnp.stack(p_kpe), jnp.stack(p_mk), jnp.stack(p_mv),
            jnp.stack(s_fk), jnp.stack(s_fv), jnp.stack(s_flf), jnp.stack(s_hs),
            jnp.stack(s_ckv), jnp.stack(s_kpe))
```

```python
import functools
import math

import jax
import jax.numpy as jnp
from jax import lax
from jax.experimental import pallas as pl
from jax.experimental.pallas import tpu as pltpu

F32 = jnp.float32
BF16 = jnp.bfloat16
EPS = 1e-6
ROPE_THETA = 10000.0
LANES = 128
VMEM_LIMIT_BYTES = 56 * 1024 * 1024
HG_CHUNK = 64
PAGES_PER_STEP = 16
NEG_INF = float("-inf")


def _params(*sem):
    return pltpu.CompilerParams(dimension_semantics=sem, vmem_limit_bytes=VMEM_LIMIT_BYTES)


def _pick(n, candidates):
    for c in candidates:
        if n % c == 0:
            return c
    raise ValueError(f"no tile for {n} in {candidates}")


def _dot(a, b):
    return jnp.dot(a, b, preferred_element_type=F32)


def _dot_nt(a, b):
    return lax.dot_general(a, b, (((1,), (1,)), ((), ())), preferred_element_type=F32)


def _dot_tn(a, b):
    return lax.dot_general(a, b, (((0,), (0,)), ((), ())), preferred_element_type=F32)


def _split3_dot(mask_bf16, x):
    x1 = x.astype(BF16)
    r1 = x - x1.astype(F32)
    x2 = r1.astype(BF16)
    x3 = (r1 - x2.astype(F32)).astype(BF16)
    return _dot(mask_bf16, x1) + _dot(mask_bf16, x2) + _dot(mask_bf16, x3)


def _norm_mm_kernel(x_ref, g_ref, w_ref, o_ref, xn_ref, *, act):
    @pl.when(pl.program_id(1) == 0)
    def _():
        x = x_ref[...]
        inv = lax.rsqrt(jnp.mean(x * x, axis=-1, keepdims=True) + EPS)
        xn_ref[...] = (x * inv * g_ref[...]).astype(BF16)

    acc = _dot(xn_ref[...], w_ref[...].astype(BF16))
    if act == "relu2":
        acc = jnp.maximum(acc, 0.0)
        acc = acc * acc
    o_ref[...] = acc.astype(o_ref.dtype)


def norm_mm(x, g, w, *, act=None, out_dtype=F32, name="norm_mm"):
    M, K = x.shape
    N = w.shape[1]
    bm = _pick(M, (1024, 512, 256))
    bn = _pick(N, (768, 512, 384, 256, 128))
    return pl.pallas_call(
        functools.partial(_norm_mm_kernel, act=act),
        grid=(M // bm, N // bn),
        in_specs=[
            pl.BlockSpec((bm, K), lambda i, j: (i, 0)),
            pl.BlockSpec((1, K), lambda i, j: (0, 0)),
            pl.BlockSpec((K, bn), lambda i, j: (0, j)),
        ],
        out_specs=pl.BlockSpec((bm, bn), lambda i, j: (i, j)),
        out_shape=jax.ShapeDtypeStruct((M, N), out_dtype),
        scratch_shapes=[pltpu.VMEM((bm, K), BF16)],
        compiler_params=_params("parallel", "arbitrary"),
        name=name,
    )(x, g.reshape(1, K).astype(F32), w)


def _mm_kernel(*refs, has_res, nk):
    a_ref, w_ref = refs[0], refs[1]
    r_ref = refs[2] if has_res else None
    o_ref = refs[3] if has_res else refs[2]
    part = _dot(a_ref[...].astype(BF16), w_ref[...].astype(BF16))
    if nk == 1:
        if has_res:
            part = part + r_ref[...]
        o_ref[...] = part.astype(o_ref.dtype)
        return
    acc_ref = refs[-1]
    k = pl.program_id(2)

    @pl.when(k == 0)
    def _():
        acc_ref[...] = part

    @pl.when(k > 0)
    def _():
        acc_ref[...] += part

    @pl.when(k == nk - 1)
    def _():
        out = acc_ref[...]
        if has_res:
            out = out + r_ref[...]
        o_ref[...] = out.astype(o_ref.dtype)


def mm(a, w, res=None, *, out_dtype=F32, name="mm"):
    M, K = a.shape
    N = w.shape[1]
    bm = _pick(M, (1024, 512, 256))
    bn = _pick(N, (512, 384, 256, 128))
    bk = K if K <= 2048 else 2048
    nk = K // bk
    in_specs = [
        pl.BlockSpec((bm, bk), lambda i, j, k: (i, k)),
        pl.BlockSpec((bk, bn), lambda i, j, k: (k, j)),
    ]
    args = [a, w]
    if res is not None:
        in_specs.append(pl.BlockSpec((bm, bn), lambda i, j, k: (i, j)))
        args.append(res)
    return pl.pallas_call(
        functools.partial(_mm_kernel, has_res=res is not None, nk=nk),
        grid=(M // bm, N // bn, nk),
        in_specs=in_specs,
        out_specs=pl.BlockSpec((bm, bn), lambda i, j, k: (i, j)),
        out_shape=jax.ShapeDtypeStruct((M, N), out_dtype),
        scratch_shapes=[pltpu.VMEM((bm, bn), F32)] if nk > 1 else [],
        compiler_params=_params("parallel", "parallel", "arbitrary"),
        name=name,
    )(*args)


def _rmsnorm_kernel(x_ref, g_ref, o_ref):
    x = x_ref[...]
    inv = lax.rsqrt(jnp.mean(x * x, axis=-1, keepdims=True) + EPS)
    o_ref[...] = x * inv * g_ref[...]


def rmsnorm_rows(x, g, name="rmsnorm"):
    M, K = x.shape
    bm = _pick(M, (512, 256))
    return pl.pallas_call(
        _rmsnorm_kernel,
        grid=(M // bm,),
        in_specs=[pl.BlockSpec((bm, K), lambda i: (i, 0)), pl.BlockSpec((1, K), lambda i: (0, 0))],
        out_specs=pl.BlockSpec((bm, K), lambda i: (i, 0)),
        out_shape=jax.ShapeDtypeStruct((M, K), F32),
        compiler_params=_params("parallel"),
        name=name,
    )(x, g.reshape(1, K).astype(F32))


def _log_sigmoid(x):
    return jnp.minimum(x, 0.0) - jnp.log(1.0 + jnp.exp(-jnp.abs(x)))


def _gate_kernel(z_ref, b_ref, lf_ref, cum_ref, carry_ref, *, br, seg):
    lf = _log_sigmoid(z_ref[...] + b_ref[...])
    lf_ref[...] = lf
    row = lax.broadcasted_iota(jnp.int32, (br, br), 0)
    col = lax.broadcasted_iota(jnp.int32, (br, br), 1)
    if seg >= br:
        mask = col <= row
    else:
        mask = (col <= row) & ((row // seg) == (col // seg))
    cum = _split3_dot(jnp.where(mask, 1.0, 0.0).astype(BF16), lf)
    if seg > br:
        blocks_per_seg = seg // br

        @pl.when(pl.program_id(0) % blocks_per_seg == 0)
        def _():
            carry_ref[...] = jnp.zeros_like(carry_ref)

        cum = cum + carry_ref[...]
        carry_ref[...] = cum[br - 1:br, :]
    cum_ref[...] = cum


def fox_gates(zf, bias, seg, name="fox_gates"):
    M = zf.shape[0]
    br = _pick(M, (512, 256, 128))
    assert seg % br == 0 or br % seg == 0
    return pl.pallas_call(
        functools.partial(_gate_kernel, br=br, seg=seg),
        grid=(M // br,),
        in_specs=[pl.BlockSpec((br, LANES), lambda i: (i, 0)), pl.BlockSpec((1, LANES), lambda i: (0, 0))],
        out_specs=[pl.BlockSpec((br, LANES), lambda i: (i, 0)), pl.BlockSpec((br, LANES), lambda i: (i, 0))],
        out_shape=[jax.ShapeDtypeStruct((M, LANES), F32), jax.ShapeDtypeStruct((M, LANES), F32)],
        scratch_shapes=[pltpu.VMEM((1, LANES), F32)],
        compiler_params=_params("arbitrary"),
        name=name,
    )(zf, bias.reshape(1, LANES))


def _flash_kernel(*refs, n_kparts, has_cum, blk, scale, nk):
    q_ref = refs[0]
    k_refs = refs[1:1 + n_kparts]
    v_ref = refs[1 + n_kparts]
    pos = 2 + n_kparts
    if has_cum:
        cq_ref, ck_ref = refs[pos], refs[pos + 1]
        pos += 2
    o_ref = refs[pos]
    m_ref, l_ref, acc_ref, q_s = refs[pos + 1:pos + 5]
    cq_s = refs[pos + 5] if has_cum else None
    h = pl.program_id(1)
    i = pl.program_id(2)
    j = pl.program_id(3)

    @pl.when(j == 0)
    def _():
        m_ref[...] = jnp.full_like(m_ref, NEG_INF)
        l_ref[...] = jnp.zeros_like(l_ref)
        acc_ref[...] = jnp.zeros_like(acc_ref)
        q_s[...] = (q_ref[...].astype(F32) * scale).astype(BF16)
        if has_cum:
            lane = lax.broadcasted_iota(jnp.int32, cq_ref.shape, 1)
            cq_s[...] = jnp.sum(jnp.where(lane == h, cq_ref[...], 0.0), axis=1, keepdims=True)

    def step(masked):
        if n_kparts == 1:
            k = k_refs[0][...].astype(BF16)
        else:
            k = jnp.concatenate([kr[...].astype(BF16) for kr in k_refs], axis=-1)
        s = _dot_nt(q_s[...], k)
        if has_cum:
            s = s + cq_s[...] - ck_ref[0]
        if masked:
            row = lax.broadcasted_iota(jnp.int32, (blk, blk), 0)
            col = lax.broadcasted_iota(jnp.int32, (blk, blk), 1)
            s = jnp.where(col <= row, s, NEG_INF)
        m_old = m_ref[...]
        m_new = jnp.maximum(m_old, jnp.max(s, axis=-1, keepdims=True))
        alpha = jnp.exp(m_old - m_new)
        p = jnp.exp(s - m_new)
        l_ref[...] = l_ref[...] * alpha + jnp.sum(p, axis=-1, keepdims=True)
        acc_ref[...] = acc_ref[...] * alpha + _dot(p.astype(BF16), v_ref[...].astype(BF16))
        m_ref[...] = m_new

    @pl.when(j < i)
    def _():
        step(False)

    @pl.when(j == i)
    def _():
        step(True)

    @pl.when(j == nk - 1)
    def _():
        o_ref[...] = (acc_ref[...] / l_ref[...]).astype(o_ref.dtype)


def flash_causal(q_arr, q_col, dq, k_parts, v_arr, v_col, dv, *, B, S, H, R, scale, cum=None, name="flash"):
    blk = 512
    nq = S // blk
    in_specs = [pl.BlockSpec((blk, dq), lambda b, h, i, j: (b * nq + i, q_col(h)))]
    args = [q_arr]
    for arr, col, width in k_parts:
        in_specs.append(pl.BlockSpec((blk, width), lambda b, h, i, j, col=col: (b * nq + jnp.minimum(i, j), col(h // R))))
        args.append(arr)
    in_specs.append(pl.BlockSpec((blk, dv), lambda b, h, i, j: (b * nq + jnp.minimum(i, j), v_col(h // R))))
    args.append(v_arr)
    scratch = [pltpu.VMEM((blk, 1), F32), pltpu.VMEM((blk, 1), F32), pltpu.VMEM((blk, dv), F32),
               pltpu.VMEM((blk, dq), BF16)]
    if cum is not None:
        cq, ck = cum
        in_specs.append(pl.BlockSpec((blk, LANES), lambda b, h, i, j: (b * nq + i, 0)))
        in_specs.append(pl.BlockSpec((1, 1, blk), lambda b, h, i, j: (b * H + h, 0, jnp.minimum(i, j))))
        args += [cq, ck]
        scratch.append(pltpu.VMEM((blk, 1), F32))
    return pl.pallas_call(
        functools.partial(_flash_kernel, n_kparts=len(k_parts), has_cum=cum is not None, blk=blk, scale=scale, nk=nq),
        grid=(B, H, nq, nq),
        in_specs=in_specs,
        out_specs=pl.BlockSpec((blk, dv), lambda b, h, i, j: (b * nq + i, h)),
        out_shape=jax.ShapeDtypeStruct((B * S, H * dv), BF16),
        scratch_shapes=scratch,
        compiler_params=_params("parallel", "parallel", "parallel", "arbitrary"),
        name=name,
    )(*args)


def _cumsum_rows(x):
    c = x.shape[0]
    row = lax.broadcasted_iota(jnp.int32, x.shape, 0)
    s = 1
    while s < c:
        x = x + jnp.where(row >= s, pltpu.roll(x, s, 0), 0.0)
        s *= 2
    return x


def _hgrn_kernel(*refs, c, nb, H, has_init, n_chunks):
    q_ref, f_ref, i_ref, g_ref, lb_ref, nw_ref = refs[:6]
    pos = 6
    s0_ref = None
    if has_init:
        s0_ref = refs[pos]
        pos += 1
    o_ref, sout_ref, st_ref = refs[pos], refs[pos + 1], refs[pos + 2]
    ci = pl.program_id(1)
    d = LANES

    @pl.when(ci == 0)
    def _():
        if has_init:
            for n in range(nb):
                for h in range(H):
                    st_ref[n, h] = s0_ref[n, h].T
        else:
            st_ref[...] = jnp.zeros_like(st_ref)

    row = lax.broadcasted_iota(jnp.int32, (c, c), 0)
    col = lax.broadcasted_iota(jnp.int32, (c, c), 1)
    causal = col <= row
    nw = nw_ref[...]
    for n in range(nb):
        rows = slice(n * c, (n + 1) * c)
        for h in range(H):
            cols = slice(h * d, (h + 1) * d)
            lb = lb_ref[:, cols]
            f = lb + (1.0 - lb) * jax.nn.sigmoid(f_ref[rows, cols])
            kk = 1.0 - f
            b = _cumsum_rows(jnp.log(f))
            b_last = b[c - 1:c, :]
            q_e = (q_ref[rows, cols] * jnp.exp(b)).astype(BF16)
            k_e = (kk * jnp.exp(-b)).astype(BF16)
            v = i_ref[rows, cols].astype(BF16)
            att = jnp.where(causal, _dot_nt(q_e, k_e), 0.0).astype(BF16)
            st = st_ref[n, h]
            o = _dot(att, v) + _dot_nt(q_e, st.astype(BF16))
            kd = (kk * jnp.exp(b_last - b)).astype(BF16)
            st_ref[n, h] = st * jnp.exp(b_last) + _dot_tn(v, kd)
            on = o * lax.rsqrt(jnp.mean(o * o, axis=-1, keepdims=True) + EPS) * nw
            gate = g_ref[rows, cols]
            o_ref[rows, cols] = (on * (gate * jax.nn.sigmoid(gate))).astype(o_ref.dtype)

    @pl.when(ci == n_chunks - 1)
    def _():
        for n in range(nb):
            for h in range(H):
                sout_ref[n, h] = st_ref[n, h].T


def hgrn2(z, cols, lb, nw, s0, *, row0, n_seq, T, nb, H, name="hgrn2"):
    c = min(HG_CHUNK, T)
    n_chunks = T // c
    W = H * LANES
    rb = nb * c
    assert (n_chunks == 1 or nb == 1) and row0 % rb == 0 and n_seq % nb == 0
    r0 = row0 // rb

    def zspec(cb):
        return pl.BlockSpec((rb, W), lambda s, ci, cb=cb: (r0 + s * n_chunks + ci, cb))

    in_specs = [zspec(cb) for cb in cols] + [
        pl.BlockSpec((1, W), lambda s, ci: (0, 0)),
        pl.BlockSpec((1, LANES), lambda s, ci: (0, 0)),
    ]
    args = [z, z, z, z, lb.reshape(1, W), nw.reshape(1, LANES)]
    if s0 is not None:
        in_specs.append(pl.BlockSpec((nb, H, LANES, LANES), lambda s, ci: (s, 0, 0, 0)))
        args.append(s0)
    return pl.pallas_call(
        functools.partial(_hgrn_kernel, c=c, nb=nb, H=H, has_init=s0 is not None, n_chunks=n_chunks),
        grid=(n_seq // nb, n_chunks),
        in_specs=in_specs,
        out_specs=[
            pl.BlockSpec((rb, W), lambda s, ci: (s * n_chunks + ci, 0)),
            pl.BlockSpec((nb, H, LANES, LANES), lambda s, ci: (s, 0, 0, 0)),
        ],
        out_shape=[jax.ShapeDtypeStruct((n_seq * T, W), BF16), jax.ShapeDtypeStruct((n_seq, H, LANES, LANES), F32)],
        scratch_shapes=[pltpu.VMEM((nb, H, LANES, LANES), F32)],
        compiler_params=_params("parallel", "arbitrary"),
        name=name,
    )(*args)


def _fox_decode_kernel(pt_ref, *refs, P, T, H, G, scale, n_steps):
    del pt_ref
    d = LANES
    q_ref, kn_ref, vn_ref, cn_ref, cnt_ref = refs[:5]
    k_refs = refs[5:5 + P]
    v_refs = refs[5 + P:5 + 2 * P]
    lf_refs = refs[5 + 2 * P:5 + 3 * P]
    o_ref = refs[5 + 3 * P]
    qbd_s, m_s, l_s, acc_s, suf_s = refs[6 + 3 * P:]
    R = H // G
    HT = H * T
    step = pl.program_id(1)

    def head_rows(x):
        return jnp.concatenate([jnp.broadcast_to(x[h:h + 1, :], (T, x.shape[1])) for h in range(H)], axis=0)

    @pl.when(step == 0)
    def _():
        q = q_ref[...] * scale
        zero = jnp.zeros((T, d), F32)
        blocks = []
        for h in range(H):
            g = h // R
            blocks.append(jnp.concatenate([q[:, h * d:(h + 1) * d] if gg == g else zero for gg in range(G)], axis=1))
        qbd = jnp.concatenate(blocks, axis=0).astype(BF16)
        qbd_s[...] = qbd
        pad = jnp.zeros((LANES - T, G * d), F32)
        kn = jnp.concatenate([kn_ref[...], pad], axis=0).astype(BF16)
        vn = jnp.concatenate([vn_ref[...], pad], axis=0).astype(BF16)
        cn = cn_ref[...]
        cn_rows = jnp.concatenate([cn[:, h:h + 1] for h in range(H)], axis=0)
        cn_cols = head_rows(cnt_ref[...])
        s = _dot_nt(qbd, kn) + cn_rows - cn_cols
        t_idx = lax.broadcasted_iota(jnp.int32, (HT, LANES), 0) % T
        s_idx = lax.broadcasted_iota(jnp.int32, (HT, LANES), 1)
        s = jnp.where(s_idx <= t_idx, s, NEG_INF)
        m = jnp.max(s, axis=-1, keepdims=True)
        p = jnp.exp(s - m)
        m_s[...] = m
        l_s[...] = jnp.sum(p, axis=-1, keepdims=True)
        acc_s[...] = _dot(p.astype(BF16), vn)
        suf_s[...] = cn_rows

    tri = (lax.broadcasted_iota(jnp.int32, (LANES, LANES), 0) <= lax.broadcasted_iota(jnp.int32, (LANES, LANES), 1))
    tri = jnp.where(tri, 1.0, 0.0).astype(BF16)
    qbd = qbd_s[...]
    suf = suf_s[...]
    s_parts = [None] * P
    for jj in range(P - 1, -1, -1):
        cw = _split3_dot_right(lf_refs[jj][...], tri)
        cw_rows = head_rows(cw)
        tot = cw_rows[:, LANES - 1:LANES]
        s = _dot_nt(qbd, k_refs[jj][...].astype(BF16))
        s_parts[jj] = s + (suf + tot) - cw_rows
        suf = suf + tot
    suf_s[...] = suf
    m_old = m_s[...]
    m_new = m_old
    for jj in range(P):
        m_new = jnp.maximum(m_new, jnp.max(s_parts[jj], axis=-1, keepdims=True))
    alpha = jnp.exp(m_old - m_new)
    l_new = l_s[...] * alpha
    acc = acc_s[...] * alpha
    for jj in range(P):
        p = jnp.exp(s_parts[jj] - m_new)
        l_new = l_new + jnp.sum(p, axis=-1, keepdims=True)
        acc = acc + _dot(p.astype(BF16), v_refs[jj][...].astype(BF16))
    m_s[...] = m_new
    l_s[...] = l_new
    acc_s[...] = acc

    @pl.when(step == n_steps - 1)
    def _():
        out = acc / l_new
        o_ref[...] = jnp.concatenate(
            [out[h * T:(h + 1) * T, (h // R) * d:(h // R + 1) * d] for h in range(H)], axis=1).astype(o_ref.dtype)


def _split3_dot_right(x, mask_bf16):
    x1 = x.astype(BF16)
    r1 = x - x1.astype(F32)
    x2 = r1.astype(BF16)
    x3 = (r1 - x2.astype(F32)).astype(BF16)
    return _dot(x1, mask_bf16) + _dot(x2, mask_bf16) + _dot(x3, mask_bf16)


def fox_decode(z, q_cb, k_cb, v_cb, cnew, cnew_t, pool_k, pool_v, pool_lft, page_table, *, row0, T, H, G, name="fox_decode"):
    Bd, n_pages = page_table.shape
    P = PAGES_PER_STEP
    n_steps = n_pages // P
    d = LANES
    page = pool_k.shape[1]
    assert page == LANES and n_pages % P == 0 and row0 % T == 0
    r0 = row0 // T

    def pool_spec(width_rows, width_cols, jj):
        def imap(b, s, pt):
            return (pt[b, (n_steps - 1 - s) * P + jj], 0, 0)
        return pl.BlockSpec((None, width_rows, width_cols), imap)

    in_specs = [
        pl.BlockSpec((T, H * d), lambda b, s, pt: (r0 + b, q_cb)),
        pl.BlockSpec((T, G * d), lambda b, s, pt: (r0 + b, k_cb)),
        pl.BlockSpec((T, G * d), lambda b, s, pt: (r0 + b, v_cb)),
        pl.BlockSpec((T, LANES), lambda b, s, pt: (b, 0)),
        pl.BlockSpec((None, H, LANES), lambda b, s, pt: (b, 0, 0)),
    ]
    in_specs += [pool_spec(page, G * d, jj) for jj in range(P)]
    in_specs += [pool_spec(page, G * d, jj) for jj in range(P)]
    in_specs += [pool_spec(H, page, jj) for jj in range(P)]
    HT = H * T
    grid_spec = pltpu.PrefetchScalarGridSpec(
        num_scalar_prefetch=1,
        grid=(Bd, n_steps),
        in_specs=in_specs,
        out_specs=pl.BlockSpec((T, H * d), lambda b, s, pt: (b, 0)),
        scratch_shapes=[pltpu.VMEM((HT, G * d), BF16), pltpu.VMEM((HT, 1), F32), pltpu.VMEM((HT, 1), F32),
                        pltpu.VMEM((HT, G * d), F32), pltpu.VMEM((HT, 1), F32)],
    )
    return pl.pallas_call(
        functools.partial(_fox_decode_kernel, P=P, T=T, H=H, G=G, scale=d ** -0.5, n_steps=n_steps),
        grid_spec=grid_spec,
        out_shape=jax.ShapeDtypeStruct((Bd * T, H * d), BF16),
        compiler_params=_params("parallel", "arbitrary"),
        name=name,
    )(page_table, z, z, z, cnew, cnew_t, *([pool_k] * P), *([pool_v] * P), *([pool_lft] * P))


def _rope_block(v, c, s1, s2):
    return v * c + pltpu.roll(v, 96, 1) * s1 + pltpu.roll(v, 32, 1) * s2


def _mla_prep_kernel(z_ref, qg_ref, kg_ref, c_ref, s1_ref, s2_ref, qa_ref, ckv_ref, kpe_ref, *, rank):
    qa = z_ref[:, :rank]
    inv = lax.rsqrt(jnp.mean(qa * qa, axis=-1, keepdims=True) + EPS)
    qa_ref[...] = (qa * inv * qg_ref[...]).astype(qa_ref.dtype)
    ckv = z_ref[:, rank:2 * rank]
    inv = lax.rsqrt(jnp.mean(ckv * ckv, axis=-1, keepdims=True) + EPS)
    ckv_ref[...] = ckv * inv * kg_ref[...]
    kpe_ref[...] = _rope_block(z_ref[:, 2 * rank:2 * rank + LANES], c_ref[...], s1_ref[...], s2_ref[...])


def mla_prep(z1, q_norm, kv_norm, rope_tabs, *, rank, name="mla_prep"):
    M, W = z1.shape
    bm = _pick(M, (1024, 512, 256))
    row = lambda i: (i, 0)
    fix = lambda i: (0, 0)
    return pl.pallas_call(
        functools.partial(_mla_prep_kernel, rank=rank),
        grid=(M // bm,),
        in_specs=[pl.BlockSpec((bm, W), row), pl.BlockSpec((1, rank), fix), pl.BlockSpec((1, rank), fix)]
        + [pl.BlockSpec((bm, LANES), row)] * 3,
        out_specs=[pl.BlockSpec((bm, rank), row), pl.BlockSpec((bm, rank), row), pl.BlockSpec((bm, LANES), row)],
        out_shape=[jax.ShapeDtypeStruct((M, rank), BF16), jax.ShapeDtypeStruct((M, rank), F32),
                   jax.ShapeDtypeStruct((M, LANES), F32)],
        compiler_params=_params("parallel"),
        name=name,
    )(z1, q_norm.reshape(1, rank), kv_norm.reshape(1, rank), *rope_tabs)


def _qproj_kernel(a_ref, w_ref, c_ref, s1_ref, s2_ref, o_ref, *, heads_per_block, scale):
    acc = _dot(a_ref[...], w_ref[...].astype(BF16)) * scale
    c, s1, s2 = c_ref[...], s1_ref[...], s2_ref[...]
    for hh in range(heads_per_block):
        base = hh * 2 * LANES
        o_ref[:, base:base + LANES] = acc[:, base:base + LANES].astype(o_ref.dtype)
        o_ref[:, base + LANES:base + 2 * LANES] = _rope_block(
            acc[:, base + LANES:base + 2 * LANES], c, s1, s2).astype(o_ref.dtype)


def mla_qproj(qa_n, w_qb_pad, rope_tabs, *, scale, name="mla_qproj"):
    M, K = qa_n.shape
    N = w_qb_pad.shape[1]
    bm = _pick(M, (1024, 512, 256))
    bn = 1024
    row = lambda i, j: (i, 0)
    return pl.pallas_call(
        functools.partial(_qproj_kernel, heads_per_block=bn // (2 * LANES), scale=scale),
        grid=(M // bm, N // bn),
        in_specs=[pl.BlockSpec((bm, K), row), pl.BlockSpec((K, bn), lambda i, j: (0, j))]
        + [pl.BlockSpec((bm, LANES), row)] * 3,
        out_specs=pl.BlockSpec((bm, bn), lambda i, j: (i, j)),
        out_shape=jax.ShapeDtypeStruct((M, N), BF16),
        compiler_params=_params("parallel", "parallel"),
        name=name,
    )(qa_n, w_qb_pad, *rope_tabs)


def _qlat_kernel(q_ref, w_ref, o_ref, *, rank):
    q = q_ref[...]
    o_ref[0, :, :rank] = _dot(q[:, :LANES], w_ref[0].astype(BF16))
    o_ref[0, :, rank:] = q[:, LANES:].astype(F32)


def mla_qlat(q_full, w_ukt, *, row0, rows, rank, name="mla_qlat"):
    Hh = w_ukt.shape[0]
    r0 = row0 // rows
    assert row0 % rows == 0
    return pl.pallas_call(
        functools.partial(_qlat_kernel, rank=rank),
        grid=(Hh,),
        in_specs=[pl.BlockSpec((rows, 2 * LANES), lambda h: (r0, h)),
                  pl.BlockSpec((1, LANES, rank), lambda h: (h, 0, 0))],
        out_specs=pl.BlockSpec((1, rows, rank + LANES), lambda h: (h, 0, 0)),
        out_shape=jax.ShapeDtypeStruct((Hh, rows, rank + LANES), F32),
        compiler_params=_params("parallel"),
        name=name,
    )(q_full, w_ukt)


def _uv_kernel(x_ref, w_ref, o_ref):
    o_ref[...] = _dot(x_ref[0].astype(BF16), w_ref[...].astype(BF16)).astype(o_ref.dtype)


def mla_uv(o_lat, w_uv2d, name="mla_uv"):
    Hh, rows, rank = o_lat.shape
    return pl.pallas_call(
        _uv_kernel,
        grid=(Hh,),
        in_specs=[pl.BlockSpec((1, rows, rank), lambda h: (h, 0, 0)), pl.BlockSpec((rank, LANES), lambda h: (0, h))],
        out_specs=pl.BlockSpec((rows, LANES), lambda h: (0, h)),
        out_shape=jax.ShapeDtypeStruct((rows, Hh * LANES), BF16),
        compiler_params=_params("parallel"),
        name=name,
    )(o_lat, w_uv2d)


def _mla_decode_kernel(pt_ref, *refs, P, T, Hh, rank, rope, n_steps):
    del pt_ref
    q_ref, cn_ref, kn_ref = refs[:3]
    c_refs = refs[3:3 + P]
    r_refs = refs[3 + P:3 + 2 * P]
    o_ref = refs[3 + 2 * P]
    ql_s, qr_s, m_s, l_s, acc_s = refs[4 + 2 * P:]
    HT = Hh * T
    step = pl.program_id(1)

    @pl.when(step == 0)
    def _():
        q = q_ref[...].reshape(HT, rank + LANES)
        ql = q[:, :rank].astype(BF16)
        qr = q[:, rank:rank + rope].astype(BF16)
        ql_s[...] = ql
        qr_s[...] = qr
        cn = jnp.concatenate([cn_ref[...], jnp.zeros((LANES - T, rank), F32)], axis=0).astype(BF16)
        kn = jnp.concatenate([kn_ref[:, :rope], jnp.zeros((LANES - T, rope), F32)], axis=0).astype(BF16)
        s = _dot_nt(ql, cn) + _dot_nt(qr, kn)
        t_idx = lax.broadcasted_iota(jnp.int32, (HT, LANES), 0) % T
        s_idx = lax.broadcasted_iota(jnp.int32, (HT, LANES), 1)
        s = jnp.where(s_idx <= t_idx, s, NEG_INF)
        m = jnp.max(s, axis=-1, keepdims=True)
        p = jnp.exp(s - m)
        m_s[...] = m
        l_s[...] = jnp.sum(p, axis=-1, keepdims=True)
        acc_s[...] = _dot(p.astype(BF16), cn)

    ql = ql_s[...]
    qr = qr_s[...]
    ckv = [c_refs[jj][...].astype(BF16) for jj in range(P)]
    s_parts = [_dot_nt(ql, ckv[jj]) + _dot_nt(qr, r_refs[jj][...].astype(BF16)) for jj in range(P)]
    m_old = m_s[...]
    m_new = m_old
    for jj in range(P):
        m_new = jnp.maximum(m_new, jnp.max(s_parts[jj], axis=-1, keepdims=True))
    alpha = jnp.exp(m_old - m_new)
    l_new = l_s[...] * alpha
    acc = acc_s[...] * alpha
    for jj in range(P):
        p = jnp.exp(s_parts[jj] - m_new)
        l_new = l_new + jnp.sum(p, axis=-1, keepdims=True)
        acc = acc + _dot(p.astype(BF16), ckv[jj])
    m_s[...] = m_new
    l_s[...] = l_new
    acc_s[...] = acc

    @pl.when(step == n_steps - 1)
    def _():
        o_ref[...] = (acc / l_new).reshape(Hh, T, rank)


def mla_decode(q_cat, ckv_n, kpe_r, pool_ckv, pool_kpe, page_table, *, row0, T, rope, name="mla_decode"):
    Hh = q_cat.shape[0]
    rank = pool_ckv.shape[-1]
    Bd, n_pages = page_table.shape
    P = PAGES_PER_STEP
    n_steps = n_pages // P
    page = pool_ckv.shape[1]
    assert page == LANES and n_pages % P == 0 and row0 % T == 0
    r0 = row0 // T

    def pool_spec(width, jj):
        return pl.BlockSpec((None, page, width), lambda b, s, pt: (pt[b, s * P + jj], 0, 0))

    in_specs = [
        pl.BlockSpec((Hh, T, rank + LANES), lambda b, s, pt: (0, b, 0)),
        pl.BlockSpec((T, rank), lambda b, s, pt: (r0 + b, 0)),
        pl.BlockSpec((T, LANES), lambda b, s, pt: (r0 + b, 0)),
    ]
    in_specs += [pool_spec(rank, jj) for jj in range(P)]
    in_specs += [pool_spec(rope, jj) for jj in range(P)]
    HT = Hh * T
    grid_spec = pltpu.PrefetchScalarGridSpec(
        num_scalar_prefetch=1,
        grid=(Bd, n_steps),
        in_specs=in_specs,
        out_specs=pl.BlockSpec((Hh, T, rank), lambda b, s, pt: (0, b, 0)),
        scratch_shapes=[pltpu.VMEM((HT, rank), BF16), pltpu.VMEM((HT, rope), BF16), pltpu.VMEM((HT, 1), F32),
                        pltpu.VMEM((HT, 1), F32), pltpu.VMEM((HT, rank), F32)],
    )
    return pl.pallas_call(
        functools.partial(_mla_decode_kernel, P=P, T=T, Hh=Hh, rank=rank, rope=rope, n_steps=n_steps),
        grid_spec=grid_spec,
        out_shape=jax.ShapeDtypeStruct((Hh, Bd * T, rank), F32),
        compiler_params=_params("parallel", "arbitrary"),
        name=name,
    )(page_table, q_cat, ckv_n, kpe_r, *([pool_ckv] * P), *([pool_kpe] * P))


def _mem_attn_kernel(q_ref, k_ref, v_ref, o_ref, *, nb, r, Hm, scale):
    d = LANES
    for n in range(nb):
        rows = slice(n * r, (n + 1) * r)
        outs = []
        for h in range(Hm):
            cols = slice(h * d, (h + 1) * d)
            q = (q_ref[rows, cols] * scale).astype(BF16)
            s = _dot_nt(q, k_ref[n, :, cols].astype(BF16))
            m = jnp.max(s, axis=-1, keepdims=True)
            p = jnp.exp(s - m)
            l = jnp.sum(p, axis=-1, keepdims=True)
            outs.append(_dot(p.astype(BF16), v_ref[n, :, cols].astype(BF16)) / l)
        o_ref[rows, :] = jnp.concatenate(outs, axis=1).astype(o_ref.dtype)


def mem_attend(q, mk, mv, *, row0, rows_per_seq, nb, r, Hm, name="mem_attn"):
    n_seq, M, W = mk.shape
    blocks_per_seq = rows_per_seq // r
    assert (nb == 1 or blocks_per_seq == 1) and row0 % (nb * r) == 0 and n_seq % nb == 0
    r0 = row0 // (nb * r)
    n_blocks = n_seq * blocks_per_seq // nb
    kv_map = lambda i: (i // blocks_per_seq, 0, 0)
    return pl.pallas_call(
        functools.partial(_mem_attn_kernel, nb=nb, r=r, Hm=Hm, scale=LANES ** -0.5),
        grid=(n_blocks,),
        in_specs=[pl.BlockSpec((nb * r, W), lambda i: (r0 + i, 0)),
                  pl.BlockSpec((nb, M, W), kv_map), pl.BlockSpec((nb, M, W), kv_map)],
        out_specs=pl.BlockSpec((nb * r, W), lambda i: (i, 0)),
        out_shape=jax.ShapeDtypeStruct((n_seq * rows_per_seq, W), BF16),
        compiler_params=_params("parallel"),
        name=name,
    )(q, mk, mv)


def _pad_cols(w, n):
    return jnp.pad(w, ((0, 0), (0, n - w.shape[1])))


def kernel(x_prompt, x_sample, mem_prompt, page_table, cache_fox_k, cache_fox_v, cache_fox_lf, state_hgrn, cache_mla_ckv, cache_mla_kpe, cache_mem_k, cache_mem_v, ev_norm, ev_w_in, ev_fox_bf, ev_hg_gamma, ev_hg_norm, ev_w_out, od_norm, od_w_in, od_q_norm, od_w_qb, od_kv_norm, od_w_uk, od_w_uv, od_w_out, xa_norm, xa_mem_norm, xa_wq, xa_wk, xa_wv, xa_wo, ff_norm, ff_w1, ff_w2, final_norm):
    B, S, D = x_prompt.shape
    Bd, T, _ = x_sample.shape
    depth = ff_w1.shape[0]
    n_pages = page_table.shape[1]
    page = cache_fox_k.shape[2]
    past_len = n_pages * page
    NP, NS = B * S, Bd * T
    N = NP + NS
    G = cache_fox_k.shape[3]
    d = cache_fox_k.shape[4]
    H = cache_fox_lf.shape[3]
    HGH = state_hgrn.shape[2]
    assert d == LANES and H * d == HGH * LANES
    W8 = H * d
    rank = cache_mla_ckv.shape[-1]
    rope = cache_mla_kpe.shape[-1]
    Hm = od_w_uk.shape[2]
    nope = od_w_uk.shape[3]
    mem_len = mem_prompt.shape[1]
    Hx = cache_mem_k.shape[3]
    XW = Hx * cache_mem_k.shape[4]
    assert nope == LANES and od_w_uv.shape[3] == LANES and rope * 2 == LANES

    x = jnp.concatenate([x_prompt.reshape(NP, D), x_sample.reshape(NS, D)], axis=0)

    half = rope // 2
    pos = jnp.concatenate([jnp.tile(jnp.arange(S), B), jnp.tile(past_len + jnp.arange(T), Bd)]).astype(F32)
    inv_freq = ROPE_THETA ** (-jnp.arange(half, dtype=F32) / half)
    ang = pos[:, None] * inv_freq[None, :]
    cos, sin = jnp.cos(ang), jnp.sin(ang)
    zero = jnp.zeros_like(cos)
    rope_tabs = (jnp.concatenate([cos, cos, zero, zero], axis=1),
                 jnp.concatenate([-sin, zero, zero, zero], axis=1),
                 jnp.concatenate([zero, sin, zero, zero], axis=1))

    outs = {k: [] for k in ("p_fk", "p_fv", "p_flf", "p_hs", "p_ckv", "p_kpe", "p_mk", "p_mv",
                            "s_fk", "s_fv", "s_flf", "s_hs", "s_ckv", "s_kpe")}
    for layer in range(depth):
        if layer % 2 == 0:
            e = layer // 2
            w = ev_w_in[e]
            o_fq, o_fk, o_fv, o_fz, o_hq = 0, W8, W8 + G * d, W8 + 2 * G * d, W8 + 2 * G * d + H
            w_re = jnp.concatenate([w[:, o_hq:], w[:, o_fq:o_fz], _pad_cols(w[:, o_fz:o_hq], LANES)], axis=1)
            c_fq, c_fk, c_fv, c_fz = 4 * W8, 5 * W8, 5 * W8 + G * d, 5 * W8 + 2 * G * d
            n_cols = c_fz + LANES
            n_pad = -(-n_cols // 768) * 768
            z = norm_mm(x, ev_norm[e], _pad_cols(w_re, n_pad), name="even_in")
            bias = jnp.pad(ev_fox_bf[e], (0, LANES - H))
            zf = z[:, c_fz:c_fz + LANES]
            lf_p, cum_p = fox_gates(zf[:NP], bias, S, name="fox_gates_prompt")
            lf_s, cum_s = fox_gates(zf[NP:], bias, T, name="fox_gates_sample")
            lb = jnp.cumsum(jax.nn.softmax(ev_hg_gamma.astype(F32), axis=0), axis=0)[e]
            ck = cum_p[:, :H].reshape(B, S, H).transpose(0, 2, 1).reshape(B * H, 1, S)
            fo_p = flash_causal(z, lambda h: c_fq // d + h, d,
                                [(z, lambda g: c_fk // d + g, d)], z, lambda g: c_fv // d + g, d,
                                B=B, S=S, H=H, R=H // G, scale=d ** -0.5, cum=(cum_p, ck), name="fox_prompt")
            hg_p, hs_p = hgrn2(z, (0, 1, 2, 3), lb, ev_hg_norm[e], None, row0=0, n_seq=B, T=S, nb=1, H=HGH,
                               name="hgrn_prompt")
            lft = cache_fox_lf[e].transpose(0, 2, 1)
            cum_st = jnp.pad(cum_s[:, :H].reshape(Bd, T, H).transpose(0, 2, 1), ((0, 0), (0, 0), (0, LANES - T)))
            fo_s = fox_decode(z, c_fq // W8, c_fk // (G * d), c_fv // (G * d), cum_s, cum_st,
                              cache_fox_k[e].reshape(-1, page, G * d), cache_fox_v[e].reshape(-1, page, G * d),
                              lft, page_table, row0=NP, T=T, H=H, G=G)
            hg_s, hs_s = hgrn2(z, (0, 1, 2, 3), lb, ev_hg_norm[e], state_hgrn[e], row0=NP, n_seq=Bd, T=T, nb=2,
                               H=HGH, name="hgrn_sample")
            cat = jnp.concatenate([jnp.concatenate([fo_p, fo_s], axis=0), jnp.concatenate([hg_p, hg_s], axis=0)], axis=1)
            x = mm(cat, ev_w_out[e], x, name="even_out")
            fk = z[:, c_fk:c_fk + G * d]
            fv = z[:, c_fv:c_fv + G * d]
            outs["p_fk"].append(fk[:NP].reshape(B, S, G, d))
            outs["p_fv"].append(fv[:NP].reshape(B, S, G, d))
            outs["p_flf"].append(lf_p[:, :H].reshape(B, S, H))
            outs["p_hs"].append(hs_p)
            outs["s_fk"].append(fk[NP:].reshape(Bd, T, G, d))
            outs["s_fv"].append(fv[NP:].reshape(Bd, T, G, d))
            outs["s_flf"].append(lf_s[:, :H].reshape(Bd, T, H))
            outs["s_hs"].append(hs_s)
        else:
            o = layer // 2
            scale = (nope + rope) ** -0.5
            n_in = 2 * rank + LANES
            z1 = norm_mm(x, od_norm[o], _pad_cols(od_w_in[o], n_in), name="mla_in")
            qa_n, ckv_n, kpe_r = mla_prep(z1, od_q_norm[o], od_kv_norm[o], rope_tabs, rank=rank)
            wq = od_w_qb[o].reshape(rank, Hm, nope + rope)
            wq = jnp.pad(wq, ((0, 0), (0, 0), (0, 2 * LANES - nope - rope))).reshape(rank, Hm * 2 * LANES)
            q_full = mla_qproj(qa_n, wq, rope_tabs, scale=scale)
            w_kv = jnp.concatenate([od_w_uk[o].reshape(rank, Hm * nope), od_w_uv[o].reshape(rank, Hm * LANES)], axis=1)
            kv = mm(ckv_n[:NP], w_kv, out_dtype=BF16, name="mla_kv_up")
            at_p = flash_causal(q_full, lambda h: h, 2 * LANES,
                                [(kv, lambda g: g, LANES), (kpe_r, lambda g: 0, LANES)], kv, lambda g: Hm + g, LANES,
                                B=B, S=S, H=Hm, R=1, scale=1.0, name="mla_prompt")
            q_cat = mla_qlat(q_full, od_w_uk[o].transpose(1, 2, 0), row0=NP, rows=NS, rank=rank)
            o_lat = mla_decode(q_cat, ckv_n, kpe_r, cache_mla_ckv[o], cache_mla_kpe[o], page_table,
                               row0=NP, T=T, rope=rope)
            at_s = mla_uv(o_lat, od_w_uv[o].reshape(rank, Hm * LANES))
            x = mm(jnp.concatenate([at_p, at_s], axis=0), od_w_out[o], x, name="mla_out")
            outs["p_ckv"].append(ckv_n[:NP].reshape(B, S, rank))
            outs["p_kpe"].append(kpe_r[:NP, :rope].reshape(B, S, rope))
            outs["s_ckv"].append(ckv_n[NP:].reshape(Bd, T, rank))
            outs["s_kpe"].append(kpe_r[NP:, :rope].reshape(Bd, T, rope))
        w_mkv = jnp.concatenate([xa_wk[layer], xa_wv[layer]], axis=1)
        mkv = norm_mm(mem_prompt.reshape(B * mem_len, D), xa_mem_norm[layer], w_mkv, name="mem_kv")
        mk, mv = mkv[:, :XW], mkv[:, XW:]
        outs["p_mk"].append(mk.reshape(B, mem_len, Hx, XW // Hx))
        outs["p_mv"].append(mv.reshape(B, mem_len, Hx, XW // Hx))
        xq = norm_mm(x, xa_norm[layer], xa_wq[layer], name="mem_q")
        xo_p = mem_attend(xq, mk.reshape(B, mem_len, XW), mv.reshape(B, mem_len, XW), row0=0, rows_per_seq=S,
                          nb=1, r=512, Hm=Hx, name="mem_attn_prompt")
        xo_s = mem_attend(xq, cache_mem_k[layer].reshape(Bd, mem_len, XW), cache_mem_v[layer].reshape(Bd, mem_len, XW),
                          row0=NP, rows_per_seq=T, nb=8, r=T, Hm=Hx, name="mem_attn_sample")
        x = mm(jnp.concatenate([xo_p, xo_s], axis=0), xa_wo[layer], x, name="mem_out")
        hmid = norm_mm(x, ff_norm[layer], ff_w1[layer], act="relu2", out_dtype=BF16, name="ff_up")
        x = mm(hmid, ff_w2[layer], x, name="ff_down")
    y = rmsnorm_rows(x, final_norm, name="final_norm")
    st = lambda k: jnp.stack(outs[k])
    return (y[:NP].reshape(B, S, D), y[NP:].reshape(Bd, T, D),
            st("p_fk"), st("p_fv"), st("p_flf"), st("p_hs"), st("p_ckv"), st("p_kpe"), st("p_mk"), st("p_mv"),
            st("s_fk"), st("s_fv"), st("s_flf"), st("s_hs"), st("s_ckv"), st("s_kpe"))
```

```python
import functools

import jax
import jax.numpy as jnp
from jax import lax
from jax.experimental import pallas as pl
from jax.experimental.pallas import tpu as pltpu

F32 = jnp.float32
BF16 = jnp.bfloat16
EPS = 1e-6
ROPE_THETA = 10000.0
LANES = 128
VMEM_LIMIT_BYTES = 56 * 1024 * 1024
HG_CHUNK = 64
FOX_PAGES_PER_STEP = 32
MLA_PAGES_PER_STEP = 16
NEG_INF = float("-inf")


def _params(*sem):
    return pltpu.CompilerParams(dimension_semantics=sem, vmem_limit_bytes=VMEM_LIMIT_BYTES)


def _pick(n, candidates):
    for c in candidates:
        if n % c == 0:
            return c
    raise ValueError(f"no tile for {n} in {candidates}")


def _dot(a, b):
    return jnp.dot(a, b, preferred_element_type=F32)


def _dot_nt(a, b):
    return lax.dot_general(a, b, (((1,), (1,)), ((), ())), preferred_element_type=F32)


def _dot_tn(a, b):
    return lax.dot_general(a, b, (((0,), (0,)), ((), ())), preferred_element_type=F32)


def _split3_dot(mask_bf16, x):
    x1 = x.astype(BF16)
    r1 = x - x1.astype(F32)
    x2 = r1.astype(BF16)
    x3 = (r1 - x2.astype(F32)).astype(BF16)
    return _dot(mask_bf16, x1) + _dot(mask_bf16, x2) + _dot(mask_bf16, x3)


def _norm_mm_kernel(x_ref, g_ref, w_ref, o_ref, xn_ref, *, act):
    @pl.when(pl.program_id(1) == 0)
    def _():
        x = x_ref[...]
        inv = lax.rsqrt(jnp.mean(x * x, axis=-1, keepdims=True) + EPS)
        xn_ref[...] = (x * inv * g_ref[...]).astype(BF16)

    acc = _dot(xn_ref[...], w_ref[...].astype(BF16))
    if act == "relu2":
        acc = jnp.maximum(acc, 0.0)
        acc = acc * acc
    o_ref[...] = acc.astype(o_ref.dtype)


def norm_mm(x, g, w, *, act=None, out_dtype=F32, name="norm_mm"):
    M, K = x.shape
    N = w.shape[1]
    bm = _pick(M, (1024, 512, 256))
    bn = _pick(N, (768, 512, 384, 256, 128))
    return pl.pallas_call(
        functools.partial(_norm_mm_kernel, act=act),
        grid=(M // bm, N // bn),
        in_specs=[
            pl.BlockSpec((bm, K), lambda i, j: (i, 0)),
            pl.BlockSpec((1, K), lambda i, j: (0, 0)),
            pl.BlockSpec((K, bn), lambda i, j: (0, j)),
        ],
        out_specs=pl.BlockSpec((bm, bn), lambda i, j: (i, j)),
        out_shape=jax.ShapeDtypeStruct((M, N), out_dtype),
        scratch_shapes=[pltpu.VMEM((bm, K), BF16)],
        compiler_params=_params("parallel", "arbitrary"),
        name=name,
    )(x, g.reshape(1, K).astype(F32), w)


def _mm_kernel(*refs, has_res, nk):
    a_ref, w_ref = refs[0], refs[1]
    r_ref = refs[2] if has_res else None
    o_ref = refs[3] if has_res else refs[2]
    part = _dot(a_ref[...].astype(BF16), w_ref[...].astype(BF16))
    if nk == 1:
        if has_res:
            part = part + r_ref[...]
        o_ref[...] = part.astype(o_ref.dtype)
        return
    acc_ref = refs[-1]
    k = pl.program_id(2)

    @pl.when(k == 0)
    def _():
        acc_ref[...] = part

    @pl.when(k > 0)
    def _():
        acc_ref[...] += part

    @pl.when(k == nk - 1)
    def _():
        out = acc_ref[...]
        if has_res:
            out = out + r_ref[...]
        o_ref[...] = out.astype(o_ref.dtype)


def mm(a, w, res=None, *, out_dtype=F32, name="mm"):
    M, K = a.shape
    N = w.shape[1]
    bm = _pick(M, (1024, 512, 256))
    bn = _pick(N, (512, 384, 256, 128))
    bk = K if K <= 2048 else 2048
    nk = K // bk
    in_specs = [
        pl.BlockSpec((bm, bk), lambda i, j, k: (i, k)),
        pl.BlockSpec((bk, bn), lambda i, j, k: (k, j)),
    ]
    args = [a, w]
    if res is not None:
        in_specs.append(pl.BlockSpec((bm, bn), lambda i, j, k: (i, j)))
        args.append(res)
    return pl.pallas_call(
        functools.partial(_mm_kernel, has_res=res is not None, nk=nk),
        grid=(M // bm, N // bn, nk),
        in_specs=in_specs,
        out_specs=pl.BlockSpec((bm, bn), lambda i, j, k: (i, j)),
        out_shape=jax.ShapeDtypeStruct((M, N), out_dtype),
        scratch_shapes=[pltpu.VMEM((bm, bn), F32)] if nk > 1 else [],
        compiler_params=_params("parallel", "parallel", "arbitrary"),
        name=name,
    )(*args)


def _rmsnorm_kernel(x_ref, g_ref, o_ref):
    x = x_ref[...]
    inv = lax.rsqrt(jnp.mean(x * x, axis=-1, keepdims=True) + EPS)
    o_ref[...] = x * inv * g_ref[...]


def rmsnorm_rows(x, g, name="rmsnorm"):
    M, K = x.shape
    bm = _pick(M, (512, 256))
    return pl.pallas_call(
        _rmsnorm_kernel,
        grid=(M // bm,),
        in_specs=[pl.BlockSpec((bm, K), lambda i: (i, 0)), pl.BlockSpec((1, K), lambda i: (0, 0))],
        out_specs=pl.BlockSpec((bm, K), lambda i: (i, 0)),
        out_shape=jax.ShapeDtypeStruct((M, K), F32),
        compiler_params=_params("parallel"),
        name=name,
    )(x, g.reshape(1, K).astype(F32))


def _log_sigmoid(x):
    return jnp.minimum(x, 0.0) - jnp.log(1.0 + jnp.exp(-jnp.abs(x)))


def _gate_kernel(z_ref, b_ref, lf_ref, cum_ref, carry_ref, *, br, seg):
    lf = _log_sigmoid(z_ref[...] + b_ref[...])
    lf_ref[...] = lf
    row = lax.broadcasted_iota(jnp.int32, (br, br), 0)
    col = lax.broadcasted_iota(jnp.int32, (br, br), 1)
    if seg >= br:
        mask = col <= row
    else:
        mask = (col <= row) & ((row // seg) == (col // seg))
    cum = _split3_dot(jnp.where(mask, 1.0, 0.0).astype(BF16), lf)
    if seg > br:
        blocks_per_seg = seg // br

        @pl.when(pl.program_id(0) % blocks_per_seg == 0)
        def _():
            carry_ref[...] = jnp.zeros_like(carry_ref)

        cum = cum + carry_ref[...]
        carry_ref[...] = cum[br - 1:br, :]
    cum_ref[...] = cum


def fox_gates(zf, bias, seg, name="fox_gates"):
    M = zf.shape[0]
    br = _pick(M, (512, 256, 128))
    assert seg % br == 0 or br % seg == 0
    return pl.pallas_call(
        functools.partial(_gate_kernel, br=br, seg=seg),
        grid=(M // br,),
        in_specs=[pl.BlockSpec((br, LANES), lambda i: (i, 0)), pl.BlockSpec((1, LANES), lambda i: (0, 0))],
        out_specs=[pl.BlockSpec((br, LANES), lambda i: (i, 0)), pl.BlockSpec((br, LANES), lambda i: (i, 0))],
        out_shape=[jax.ShapeDtypeStruct((M, LANES), F32), jax.ShapeDtypeStruct((M, LANES), F32)],
        scratch_shapes=[pltpu.VMEM((1, LANES), F32)],
        compiler_params=_params("arbitrary"),
        name=name,
    )(zf, bias.reshape(1, LANES))


def _flash_kernel(*refs, n_kparts, has_cum, blk, scale, nk, n_sub):
    q_ref = refs[0]
    k_refs = refs[1:1 + n_kparts]
    v_ref = refs[1 + n_kparts]
    pos = 2 + n_kparts
    if has_cum:
        cq_ref, ck_ref = refs[pos], refs[pos + 1]
        pos += 2
    o_ref = refs[pos]
    m_ref, l_ref, acc_ref, qt_s = refs[pos + 1:pos + 5]
    h = pl.program_id(1)
    i = pl.program_id(2)
    j = pl.program_id(3)
    hw = blk // n_sub

    @pl.when(j == 0)
    def _():
        m_ref[...] = jnp.full_like(m_ref, NEG_INF)
        l_ref[...] = jnp.zeros_like(l_ref)
        acc_ref[...] = jnp.zeros_like(acc_ref)
        qt_s[...] = (q_ref[...].astype(F32) * scale).T.astype(BF16)

    def step(masked):
        if n_kparts == 1:
            k = k_refs[0][...].astype(BF16)
        else:
            k = jnp.concatenate([kr[...].astype(BF16) for kr in k_refs], axis=-1)
        v = v_ref[...].astype(BF16)
        if has_cum:
            sel = lax.broadcasted_iota(jnp.int32, (LANES, LANES), 0) == h
            ck_rep = _split3_dot_right(ck_ref[...], jnp.where(sel, 1.0, 0.0).astype(BF16))
            ck_rep = jnp.concatenate([ck_rep] * (hw // LANES), axis=1)
        for sub in range(n_sub):
            cols = slice(sub * hw, (sub + 1) * hw)
            st = _dot(k, qt_s[:, cols])
            if has_cum:
                st = st + cq_ref[0][:, cols] - ck_rep
            if masked:
                key = lax.broadcasted_iota(jnp.int32, (blk, hw), 0)
                qry = lax.broadcasted_iota(jnp.int32, (blk, hw), 1) + sub * hw
                st = jnp.where(key <= qry, st, NEG_INF)
            m_old = m_ref[:, cols]
            m_new = jnp.maximum(m_old, jnp.max(st, axis=0, keepdims=True))
            alpha = jnp.exp(m_old - m_new)
            pt = jnp.exp(st - m_new)
            l_ref[:, cols] = l_ref[:, cols] * alpha + jnp.sum(pt, axis=0, keepdims=True)
            acc_ref[:, cols] = acc_ref[:, cols] * alpha + _dot_tn(v, pt.astype(BF16))
            m_ref[:, cols] = m_new

    @pl.when(j < i)
    def _():
        step(False)

    @pl.when(j == i)
    def _():
        step(True)

    @pl.when(j == nk - 1)
    def _():
        o_ref[...] = (acc_ref[...] / l_ref[...]).T.astype(o_ref.dtype)


def flash_causal(q_arr, q_col, dq, k_parts, v_arr, v_col, dv, *, B, S, H, R, scale, cum=None, name="flash"):
    blk = 512
    n_sub = 2
    nq = S // blk
    in_specs = [pl.BlockSpec((blk, dq), lambda b, h, i, j: (b * nq + i, q_col(h)))]
    args = [q_arr]
    for arr, col, width in k_parts:
        in_specs.append(pl.BlockSpec((blk, width), lambda b, h, i, j, col=col: (b * nq + jnp.minimum(i, j), col(h // R))))
        args.append(arr)
    in_specs.append(pl.BlockSpec((blk, dv), lambda b, h, i, j: (b * nq + jnp.minimum(i, j), v_col(h // R))))
    args.append(v_arr)
    scratch = [pltpu.VMEM((1, blk), F32), pltpu.VMEM((1, blk), F32), pltpu.VMEM((dv, blk), F32),
               pltpu.VMEM((dq, blk), BF16)]
    if cum is not None:
        cq, ck = cum
        in_specs.append(pl.BlockSpec((1, 1, blk), lambda b, h, i, j: (b * H + h, 0, i)))
        in_specs.append(pl.BlockSpec((blk, LANES), lambda b, h, i, j: (b * nq + jnp.minimum(i, j), 0)))
        args += [cq, ck]
    return pl.pallas_call(
        functools.partial(_flash_kernel, n_kparts=len(k_parts), has_cum=cum is not None, blk=blk, scale=scale, nk=nq,
                          n_sub=n_sub),
        grid=(B, H, nq, nq),
        in_specs=in_specs,
        out_specs=pl.BlockSpec((blk, dv), lambda b, h, i, j: (b * nq + i, h)),
        out_shape=jax.ShapeDtypeStruct((B * S, H * dv), BF16),
        scratch_shapes=scratch,
        compiler_params=_params("parallel", "parallel", "parallel", "arbitrary"),
        name=name,
    )(*args)


def _cumsum_rows(x):
    c = x.shape[0]
    row = lax.broadcasted_iota(jnp.int32, x.shape, 0)
    s = 1
    while s < c:
        x = x + jnp.where(row >= s, pltpu.roll(x, s, 0), 0.0)
        s *= 2
    return x


def _hgrn_kernel(*refs, c, nb, H, has_init, n_chunks):
    q_ref, f_ref, i_ref, g_ref, lb_ref, nw_ref = refs[:6]
    pos = 6
    s0_ref = None
    if has_init:
        s0_ref = refs[pos]
        pos += 1
    o_ref, sout_ref, st_ref = refs[pos], refs[pos + 1], refs[pos + 2]
    ci = pl.program_id(1)
    d = LANES

    @pl.when(ci == 0)
    def _():
        if has_init:
            for n in range(nb):
                for h in range(H):
                    st_ref[n, h] = s0_ref[n, h].T
        else:
            st_ref[...] = jnp.zeros_like(st_ref)

    row = lax.broadcasted_iota(jnp.int32, (c, c), 0)
    col = lax.broadcasted_iota(jnp.int32, (c, c), 1)
    causal = col <= row
    nw = nw_ref[...]
    for n in range(nb):
        rows = slice(n * c, (n + 1) * c)
        for h in range(H):
            cols = slice(h * d, (h + 1) * d)
            lb = lb_ref[:, cols]
            f = lb + (1.0 - lb) * jax.nn.sigmoid(f_ref[rows, cols])
            kk = 1.0 - f
            b = _cumsum_rows(jnp.log(f))
            b_last = b[c - 1:c, :]
            q_e = (q_ref[rows, cols] * jnp.exp(b)).astype(BF16)
            k_e = (kk * jnp.exp(-b)).astype(BF16)
            v = i_ref[rows, cols].astype(BF16)
            att = jnp.where(causal, _dot_nt(q_e, k_e), 0.0).astype(BF16)
            st = st_ref[n, h]
            o = _dot(att, v) + _dot_nt(q_e, st.astype(BF16))
            kd = (kk * jnp.exp(b_last - b)).astype(BF16)
            st_ref[n, h] = st * jnp.exp(b_last) + _dot_tn(v, kd)
            on = o * lax.rsqrt(jnp.mean(o * o, axis=-1, keepdims=True) + EPS) * nw
            gate = g_ref[rows, cols]
            o_ref[rows, cols] = (on * (gate * jax.nn.sigmoid(gate))).astype(o_ref.dtype)

    @pl.when(ci == n_chunks - 1)
    def _():
        for n in range(nb):
            for h in range(H):
                sout_ref[n, h] = st_ref[n, h].T


def hgrn2(z, cols, lb, nw, s0, *, row0, n_seq, T, nb, H, name="hgrn2"):
    c = min(HG_CHUNK, T)
    n_chunks = T // c
    W = H * LANES
    rb = nb * c
    assert (n_chunks == 1 or nb == 1) and row0 % rb == 0 and n_seq % nb == 0
    r0 = row0 // rb

    def zspec(cb):
        return pl.BlockSpec((rb, W), lambda s, ci, cb=cb: (r0 + s * n_chunks + ci, cb))

    in_specs = [zspec(cb) for cb in cols] + [
        pl.BlockSpec((1, W), lambda s, ci: (0, 0)),
        pl.BlockSpec((1, LANES), lambda s, ci: (0, 0)),
    ]
    args = [z, z, z, z, lb.reshape(1, W), nw.reshape(1, LANES)]
    if s0 is not None:
        in_specs.append(pl.BlockSpec((nb, H, LANES, LANES), lambda s, ci: (s, 0, 0, 0)))
        args.append(s0)
    return pl.pallas_call(
        functools.partial(_hgrn_kernel, c=c, nb=nb, H=H, has_init=s0 is not None, n_chunks=n_chunks),
        grid=(n_seq // nb, n_chunks),
        in_specs=in_specs,
        out_specs=[
            pl.BlockSpec((rb, W), lambda s, ci: (s * n_chunks + ci, 0)),
            pl.BlockSpec((nb, H, LANES, LANES), lambda s, ci: (s, 0, 0, 0)),
        ],
        out_shape=[jax.ShapeDtypeStruct((n_seq * T, W), BF16), jax.ShapeDtypeStruct((n_seq, H, LANES, LANES), F32)],
        scratch_shapes=[pltpu.VMEM((nb, H, LANES, LANES), F32)],
        compiler_params=_params("parallel", "arbitrary"),
        name=name,
    )(*args)


def _page_copies(pt_ref, pools, bufs, sems, b, chunk, slot, P):
    out = []
    for jj in range(P):
        pg = pt_ref[b, chunk * P + jj]
        for a, (pool, buf) in enumerate(zip(pools, bufs)):
            out.append(pltpu.make_async_copy(pool.at[pg], buf.at[slot, jj], sems.at[a, slot]))
    return out


def _gather_pages(pt_ref, pools, bufs, sems, *, P, n_steps, n_seq, chunk_of_step):
    b = pl.program_id(0)
    step = pl.program_id(1)
    g = b * n_steps + step
    slot = lax.rem(g, 2)

    @pl.when(g == 0)
    def _():
        for c in _page_copies(pt_ref, pools, bufs, sems, 0, chunk_of_step(0), 0, P):
            c.start()

    @pl.when(g + 1 < n_seq * n_steps)
    def _():
        nxt = g + 1
        nb = nxt // n_steps
        for c in _page_copies(pt_ref, pools, bufs, sems, nb, chunk_of_step(nxt - nb * n_steps), 1 - slot, P):
            c.start()

    for c in _page_copies(pt_ref, pools, bufs, sems, b, chunk_of_step(step), slot, P):
        c.wait()
    return slot


def _fox_decode_kernel(pt_ref, q_ref, kn_ref, vn_ref, cn_ref, cnt_ref, k_hbm, v_hbm, lf_hbm, o_ref,
                       kbuf, vbuf, lfbuf, sems, qbd_s, m_s, l_s, acc_s, suf_s, *, P, T, H, G, scale, n_steps, n_seq):
    d = LANES
    R = H // G
    HT = H * T
    step = pl.program_id(1)
    slot = _gather_pages(pt_ref, (k_hbm, v_hbm, lf_hbm), (kbuf, vbuf, lfbuf), sems, P=P, n_steps=n_steps,
                         n_seq=n_seq, chunk_of_step=lambda s: n_steps - 1 - s)

    def page_rows(buf, jj):
        ref = buf.at[slot, jj]
        return jnp.concatenate([ref[pl.ds(g, LANES, stride=G), :] for g in range(G)], axis=1).astype(BF16)

    def head_rows(x):
        return jnp.concatenate([jnp.broadcast_to(x[h:h + 1, :], (T, x.shape[1])) for h in range(H)], axis=0)

    @pl.when(step == 0)
    def _():
        q = q_ref[...] * scale
        zero = jnp.zeros((T, d), F32)
        blocks = []
        for h in range(H):
            g = h // R
            blocks.append(jnp.concatenate([q[:, h * d:(h + 1) * d] if gg == g else zero for gg in range(G)], axis=1))
        qbd = jnp.concatenate(blocks, axis=0).astype(BF16)
        qbd_s[...] = qbd
        pad = jnp.zeros((LANES - T, G * d), F32)
        kn = jnp.concatenate([kn_ref[...], pad], axis=0).astype(BF16)
        vn = jnp.concatenate([vn_ref[...], pad], axis=0).astype(BF16)
        cn = cn_ref[...]
        cn_rows = jnp.concatenate([cn[:, h:h + 1] for h in range(H)], axis=0)
        cn_cols = head_rows(cnt_ref[...])
        s = _dot_nt(qbd, kn) + cn_rows - cn_cols
        t_idx = lax.broadcasted_iota(jnp.int32, (HT, LANES), 0) % T
        s_idx = lax.broadcasted_iota(jnp.int32, (HT, LANES), 1)
        s = jnp.where(s_idx <= t_idx, s, NEG_INF)
        m = jnp.max(s, axis=-1, keepdims=True)
        p = jnp.exp(s - m)
        m_s[...] = m
        l_s[...] = jnp.sum(p, axis=-1, keepdims=True)
        acc_s[...] = _dot(p.astype(BF16), vn)
        suf_s[...] = cn_rows

    tri = (lax.broadcasted_iota(jnp.int32, (LANES, LANES), 0) <= lax.broadcasted_iota(jnp.int32, (LANES, LANES), 1))
    tri = jnp.where(tri, 1.0, 0.0).astype(BF16)
    qbd = qbd_s[...]
    suf = suf_s[...]
    cw_all = _split3_dot_right(lfbuf[slot].reshape(P * H, LANES), tri)
    s_parts = [None] * P
    for jj in range(P - 1, -1, -1):
        cw_rows = head_rows(cw_all[jj * H:(jj + 1) * H, :])
        tot = cw_rows[:, LANES - 1:LANES]
        s = _dot_nt(qbd, page_rows(kbuf, jj))
        s_parts[jj] = s + (suf + tot) - cw_rows
        suf = suf + tot
    suf_s[...] = suf
    m_old = m_s[...]
    m_new = m_old
    for jj in range(P):
        m_new = jnp.maximum(m_new, jnp.max(s_parts[jj], axis=-1, keepdims=True))
    alpha = jnp.exp(m_old - m_new)
    l_new = l_s[...] * alpha
    acc = acc_s[...] * alpha
    for jj in range(P):
        p = jnp.exp(s_parts[jj] - m_new)
        l_new = l_new + jnp.sum(p, axis=-1, keepdims=True)
        acc = acc + _dot(p.astype(BF16), page_rows(vbuf, jj))
    m_s[...] = m_new
    l_s[...] = l_new
    acc_s[...] = acc

    @pl.when(step == n_steps - 1)
    def _():
        out = acc / l_new
        o_ref[...] = jnp.concatenate(
            [out[h * T:(h + 1) * T, (h // R) * d:(h // R + 1) * d] for h in range(H)], axis=1).astype(o_ref.dtype)


def _split3_dot_right(x, mask_bf16):
    x1 = x.astype(BF16)
    r1 = x - x1.astype(F32)
    x2 = r1.astype(BF16)
    x3 = (r1 - x2.astype(F32)).astype(BF16)
    return _dot(x1, mask_bf16) + _dot(x2, mask_bf16) + _dot(x3, mask_bf16)


def fox_decode(z, q_cb, k_cb, v_cb, cnew, cnew_t, pool_k, pool_v, pool_lft, page_table, *, row0, T, H, G, name="fox_decode"):
    Bd, n_pages = page_table.shape
    P = FOX_PAGES_PER_STEP
    n_steps = n_pages // P
    d = LANES
    page = pool_lft.shape[2]
    assert page == LANES and pool_k.shape[1] == page * G and n_pages % P == 0 and row0 % T == 0
    r0 = row0 // T
    hbm = pl.BlockSpec(memory_space=pl.ANY)
    in_specs = [
        pl.BlockSpec((T, H * d), lambda b, s, pt: (r0 + b, q_cb)),
        pl.BlockSpec((T, G * d), lambda b, s, pt: (r0 + b, k_cb)),
        pl.BlockSpec((T, G * d), lambda b, s, pt: (r0 + b, v_cb)),
        pl.BlockSpec((T, LANES), lambda b, s, pt: (b, 0)),
        pl.BlockSpec((None, H, LANES), lambda b, s, pt: (b, 0, 0)),
        hbm, hbm, hbm,
    ]
    HT = H * T
    grid_spec = pltpu.PrefetchScalarGridSpec(
        num_scalar_prefetch=1,
        grid=(Bd, n_steps),
        in_specs=in_specs,
        out_specs=pl.BlockSpec((T, H * d), lambda b, s, pt: (b, 0)),
        scratch_shapes=[pltpu.VMEM((2, P, page * G, d), F32), pltpu.VMEM((2, P, page * G, d), F32),
                        pltpu.VMEM((2, P, H, page), F32), pltpu.SemaphoreType.DMA((3, 2)),
                        pltpu.VMEM((HT, G * d), BF16), pltpu.VMEM((HT, 1), F32), pltpu.VMEM((HT, 1), F32),
                        pltpu.VMEM((HT, G * d), F32), pltpu.VMEM((HT, 1), F32)],
    )
    return pl.pallas_call(
        functools.partial(_fox_decode_kernel, P=P, T=T, H=H, G=G, scale=d ** -0.5, n_steps=n_steps, n_seq=Bd),
        grid_spec=grid_spec,
        out_shape=jax.ShapeDtypeStruct((Bd * T, H * d), BF16),
        compiler_params=_params("arbitrary", "arbitrary"),
        name=name,
    )(page_table, z, z, z, cnew, cnew_t, pool_k, pool_v, pool_lft)


def _rope_block(v, c, s1, s2):
    return v * c + pltpu.roll(v, 96, 1) * s1 + pltpu.roll(v, 32, 1) * s2


def _mla_prep_kernel(z_ref, qg_ref, kg_ref, c_ref, s1_ref, s2_ref, qa_ref, ckv_ref, kpe_ref, *, rank):
    qa = z_ref[:, :rank]
    inv = lax.rsqrt(jnp.mean(qa * qa, axis=-1, keepdims=True) + EPS)
    qa_ref[...] = (qa * inv * qg_ref[...]).astype(qa_ref.dtype)
    ckv = z_ref[:, rank:2 * rank]
    inv = lax.rsqrt(jnp.mean(ckv * ckv, axis=-1, keepdims=True) + EPS)
    ckv_ref[...] = ckv * inv * kg_ref[...]
    kpe_ref[...] = _rope_block(z_ref[:, 2 * rank:2 * rank + LANES], c_ref[...], s1_ref[...], s2_ref[...])


def mla_prep(z1, q_norm, kv_norm, rope_tabs, *, rank, name="mla_prep"):
    M, W = z1.shape
    bm = _pick(M, (1024, 512, 256))
    row = lambda i: (i, 0)
    fix = lambda i: (0, 0)
    return pl.pallas_call(
        functools.partial(_mla_prep_kernel, rank=rank),
        grid=(M // bm,),
        in_specs=[pl.BlockSpec((bm, W), row), pl.BlockSpec((1, rank), fix), pl.BlockSpec((1, rank), fix)]
        + [pl.BlockSpec((bm, LANES), row)] * 3,
        out_specs=[pl.BlockSpec((bm, rank), row), pl.BlockSpec((bm, rank), row), pl.BlockSpec((bm, LANES), row)],
        out_shape=[jax.ShapeDtypeStruct((M, rank), BF16), jax.ShapeDtypeStruct((M, rank), F32),
                   jax.ShapeDtypeStruct((M, LANES), F32)],
        compiler_params=_params("parallel"),
        name=name,
    )(z1, q_norm.reshape(1, rank), kv_norm.reshape(1, rank), *rope_tabs)


def _qproj_kernel(a_ref, w_ref, c_ref, s1_ref, s2_ref, o_ref, *, heads_per_block, scale):
    acc = _dot(a_ref[...], w_ref[...].astype(BF16)) * scale
    c, s1, s2 = c_ref[...], s1_ref[...], s2_ref[...]
    for hh in range(heads_per_block):
        base = hh * 2 * LANES
        o_ref[:, base:base + LANES] = acc[:, base:base + LANES].astype(o_ref.dtype)
        o_ref[:, base + LANES:base + 2 * LANES] = _rope_block(
            acc[:, base + LANES:base + 2 * LANES], c, s1, s2).astype(o_ref.dtype)


def mla_qproj(qa_n, w_qb_pad, rope_tabs, *, scale, name="mla_qproj"):
    M, K = qa_n.shape
    N = w_qb_pad.shape[1]
    bm = _pick(M, (1024, 512, 256))
    bn = 1024
    row = lambda i, j: (i, 0)
    return pl.pallas_call(
        functools.partial(_qproj_kernel, heads_per_block=bn // (2 * LANES), scale=scale),
        grid=(M // bm, N // bn),
        in_specs=[pl.BlockSpec((bm, K), row), pl.BlockSpec((K, bn), lambda i, j: (0, j))]
        + [pl.BlockSpec((bm, LANES), row)] * 3,
        out_specs=pl.BlockSpec((bm, bn), lambda i, j: (i, j)),
        out_shape=jax.ShapeDtypeStruct((M, N), BF16),
        compiler_params=_params("parallel", "parallel"),
        name=name,
    )(qa_n, w_qb_pad, *rope_tabs)


def _qlat_kernel(q_ref, w_ref, o_ref, *, rank):
    q = q_ref[...]
    o_ref[0, :, :rank] = _dot(q[:, :LANES], w_ref[0].astype(BF16))
    o_ref[0, :, rank:] = q[:, LANES:].astype(F32)


def mla_qlat(q_full, w_ukt, *, row0, rows, rank, name="mla_qlat"):
    Hh = w_ukt.shape[0]
    r0 = row0 // rows
    assert row0 % rows == 0
    return pl.pallas_call(
        functools.partial(_qlat_kernel, rank=rank),
        grid=(Hh,),
        in_specs=[pl.BlockSpec((rows, 2 * LANES), lambda h: (r0, h)),
                  pl.BlockSpec((1, LANES, rank), lambda h: (h, 0, 0))],
        out_specs=pl.BlockSpec((1, rows, rank + LANES), lambda h: (h, 0, 0)),
        out_shape=jax.ShapeDtypeStruct((Hh, rows, rank + LANES), F32),
        compiler_params=_params("parallel"),
        name=name,
    )(q_full, w_ukt)


def _uv_kernel(x_ref, w_ref, o_ref):
    o_ref[...] = _dot(x_ref[0].astype(BF16), w_ref[...].astype(BF16)).astype(o_ref.dtype)


def mla_uv(o_lat, w_uv2d, name="mla_uv"):
    Hh, rows, rank = o_lat.shape
    return pl.pallas_call(
        _uv_kernel,
        grid=(Hh,),
        in_specs=[pl.BlockSpec((1, rows, rank), lambda h: (h, 0, 0)), pl.BlockSpec((rank, LANES), lambda h: (0, h))],
        out_specs=pl.BlockSpec((rows, LANES), lambda h: (0, h)),
        out_shape=jax.ShapeDtypeStruct((rows, Hh * LANES), BF16),
        compiler_params=_params("parallel"),
        name=name,
    )(o_lat, w_uv2d)


def _mla_decode_kernel(pt_ref, q_ref, cn_ref, kn_ref, c_hbm, r_hbm, o_ref, cbuf, rbuf, sems,
                       ql_s, qr_s, m_s, l_s, acc_s, *, P, T, Hh, rank, rope, n_steps, n_seq):
    HT = Hh * T
    step = pl.program_id(1)
    slot = _gather_pages(pt_ref, (c_hbm, r_hbm), (cbuf, rbuf), sems, P=P, n_steps=n_steps, n_seq=n_seq,
                         chunk_of_step=lambda s: s)

    @pl.when(step == 0)
    def _():
        q = q_ref[...].reshape(HT, rank + LANES)
        ql = q[:, :rank].astype(BF16)
        qr = q[:, rank:rank + rope].astype(BF16)
        ql_s[...] = ql
        qr_s[...] = qr
        cn = jnp.concatenate([cn_ref[...], jnp.zeros((LANES - T, rank), F32)], axis=0).astype(BF16)
        kn = jnp.concatenate([kn_ref[:, :rope], jnp.zeros((LANES - T, rope), F32)], axis=0).astype(BF16)
        s = _dot_nt(ql, cn) + _dot_nt(qr, kn)
        t_idx = lax.broadcasted_iota(jnp.int32, (HT, LANES), 0) % T
        s_idx = lax.broadcasted_iota(jnp.int32, (HT, LANES), 1)
        s = jnp.where(s_idx <= t_idx, s, NEG_INF)
        m = jnp.max(s, axis=-1, keepdims=True)
        p = jnp.exp(s - m)
        m_s[...] = m
        l_s[...] = jnp.sum(p, axis=-1, keepdims=True)
        acc_s[...] = _dot(p.astype(BF16), cn)

    ql = ql_s[...]
    qr = qr_s[...]
    n_pairs = P // 2
    ckv = [jnp.concatenate([cbuf[slot, 2 * jj].astype(BF16), cbuf[slot, 2 * jj + 1].astype(BF16)], axis=0)
           for jj in range(n_pairs)]
    s_parts = []
    for jj in range(n_pairs):
        kpe_t = jnp.concatenate([rbuf[slot, 2 * jj], rbuf[slot, 2 * jj + 1]], axis=1).astype(BF16)
        s_parts.append(_dot_nt(ql, ckv[jj]) + _dot(qr, kpe_t))
    m_old = m_s[...]
    m_new = m_old
    for jj in range(n_pairs):
        m_new = jnp.maximum(m_new, jnp.max(s_parts[jj], axis=-1, keepdims=True))
    alpha = jnp.exp(m_old - m_new)
    l_new = l_s[...] * alpha
    acc = acc_s[...] * alpha
    for jj in range(n_pairs):
        p = jnp.exp(s_parts[jj] - m_new)
        l_new = l_new + jnp.sum(p, axis=-1, keepdims=True)
        acc = acc + _dot(p.astype(BF16), ckv[jj])
    m_s[...] = m_new
    l_s[...] = l_new
    acc_s[...] = acc

    @pl.when(step == n_steps - 1)
    def _():
        o_ref[...] = (acc / l_new).reshape(Hh, T, rank)


def mla_decode(q_cat, ckv_n, kpe_r, pool_ckv, pool_kpe_t, page_table, *, row0, T, rope, name="mla_decode"):
    Hh = q_cat.shape[0]
    rank = pool_ckv.shape[-1]
    Bd, n_pages = page_table.shape
    P = MLA_PAGES_PER_STEP
    n_steps = n_pages // P
    page = pool_ckv.shape[1]
    assert page == LANES and pool_kpe_t.shape[1:] == (rope, page) and n_pages % P == 0 and P % 2 == 0 and row0 % T == 0
    r0 = row0 // T
    hbm = pl.BlockSpec(memory_space=pl.ANY)
    in_specs = [
        pl.BlockSpec((Hh, T, rank + LANES), lambda b, s, pt: (0, b, 0)),
        pl.BlockSpec((T, rank), lambda b, s, pt: (r0 + b, 0)),
        pl.BlockSpec((T, LANES), lambda b, s, pt: (r0 + b, 0)),
        hbm, hbm,
    ]
    HT = Hh * T
    grid_spec = pltpu.PrefetchScalarGridSpec(
        num_scalar_prefetch=1,
        grid=(Bd, n_steps),
        in_specs=in_specs,
        out_specs=pl.BlockSpec((Hh, T, rank), lambda b, s, pt: (0, b, 0)),
        scratch_shapes=[pltpu.VMEM((2, P, page, rank), F32), pltpu.VMEM((2, P, rope, page), F32),
                        pltpu.SemaphoreType.DMA((2, 2)),
                        pltpu.VMEM((HT, rank), BF16), pltpu.VMEM((HT, rope), BF16), pltpu.VMEM((HT, 1), F32),
                        pltpu.VMEM((HT, 1), F32), pltpu.VMEM((HT, rank), F32)],
    )
    return pl.pallas_call(
        functools.partial(_mla_decode_kernel, P=P, T=T, Hh=Hh, rank=rank, rope=rope, n_steps=n_steps, n_seq=Bd),
        grid_spec=grid_spec,
        out_shape=jax.ShapeDtypeStruct((Hh, Bd * T, rank), F32),
        compiler_params=_params("arbitrary", "arbitrary"),
        name=name,
    )(page_table, q_cat, ckv_n, kpe_r, pool_ckv, pool_kpe_t)


def _mem_attn_kernel(q_ref, k_ref, v_ref, o_ref, *, nb, r, Hm, scale, head_rows, mem_len):
    d = LANES
    pairs = [(n, h) for n in range(nb) for h in range(Hm)]

    def mem(ref, n, h):
        if head_rows:
            return ref[n, pl.ds(h, mem_len, stride=Hm), :].astype(BF16)
        return ref[n, :, h * d:(h + 1) * d].astype(BF16)

    q_all = (q_ref[...] * scale).astype(BF16)
    s_all = [_dot_nt(q_all[n * r:(n + 1) * r, h * d:(h + 1) * d], mem(k_ref, n, h)) for n, h in pairs]
    p_all, l_all = [], []
    for s in s_all:
        p = jnp.exp(s - jnp.max(s, axis=-1, keepdims=True))
        p_all.append(p.astype(BF16))
        l_all.append(jnp.sum(p, axis=-1, keepdims=True))
    o_all = [_dot(p, mem(v_ref, n, h)) / l for (n, h), p, l in zip(pairs, p_all, l_all)]
    o_ref[...] = jnp.concatenate(
        [jnp.concatenate(o_all[n * Hm:(n + 1) * Hm], axis=1) for n in range(nb)], axis=0).astype(o_ref.dtype)


def mem_attend(q, mk, mv, *, row0, rows_per_seq, nb, r, Hm, head_rows, name="mem_attn"):
    n_seq, M, W = mk.shape
    mem_len = M // Hm if head_rows else M
    blocks_per_seq = rows_per_seq // r
    assert (nb == 1 or blocks_per_seq == 1) and row0 % (nb * r) == 0 and n_seq % nb == 0
    r0 = row0 // (nb * r)
    n_blocks = n_seq * blocks_per_seq // nb
    kv_map = lambda i: (i // blocks_per_seq, 0, 0)
    QW = Hm * LANES
    return pl.pallas_call(
        functools.partial(_mem_attn_kernel, nb=nb, r=r, Hm=Hm, scale=LANES ** -0.5, head_rows=head_rows,
                          mem_len=mem_len),
        grid=(n_blocks,),
        in_specs=[pl.BlockSpec((nb * r, QW), lambda i: (r0 + i, 0)),
                  pl.BlockSpec((nb, M, W), kv_map), pl.BlockSpec((nb, M, W), kv_map)],
        out_specs=pl.BlockSpec((nb * r, QW), lambda i: (i, 0)),
        out_shape=jax.ShapeDtypeStruct((n_seq * rows_per_seq, QW), BF16),
        compiler_params=_params("parallel"),
        name=name,
    )(q, mk, mv)


def _pad_cols(w, n):
    return jnp.pad(w, ((0, 0), (0, n - w.shape[1])))


def kernel(x_prompt, x_sample, mem_prompt, page_table, cache_fox_k, cache_fox_v, cache_fox_lf, state_hgrn, cache_mla_ckv, cache_mla_kpe, cache_mem_k, cache_mem_v, ev_norm, ev_w_in, ev_fox_bf, ev_hg_gamma, ev_hg_norm, ev_w_out, od_norm, od_w_in, od_q_norm, od_w_qb, od_kv_norm, od_w_uk, od_w_uv, od_w_out, xa_norm, xa_mem_norm, xa_wq, xa_wk, xa_wv, xa_wo, ff_norm, ff_w1, ff_w2, final_norm):
    B, S, D = x_prompt.shape
    Bd, T, _ = x_sample.shape
    depth = ff_w1.shape[0]
    n_pages = page_table.shape[1]
    page = cache_fox_k.shape[2]
    past_len = n_pages * page
    NP, NS = B * S, Bd * T
    N = NP + NS
    G = cache_fox_k.shape[3]
    d = cache_fox_k.shape[4]
    H = cache_fox_lf.shape[3]
    HGH = state_hgrn.shape[2]
    assert d == LANES and H * d == HGH * LANES
    W8 = H * d
    rank = cache_mla_ckv.shape[-1]
    rope = cache_mla_kpe.shape[-1]
    Hm = od_w_uk.shape[2]
    nope = od_w_uk.shape[3]
    mem_len = mem_prompt.shape[1]
    Hx = cache_mem_k.shape[3]
    XW = Hx * cache_mem_k.shape[4]
    assert nope == LANES and od_w_uv.shape[3] == LANES and rope * 2 == LANES

    x = jnp.concatenate([x_prompt.reshape(NP, D), x_sample.reshape(NS, D)], axis=0)

    half = rope // 2
    pos = jnp.concatenate([jnp.tile(jnp.arange(S), B), jnp.tile(past_len + jnp.arange(T), Bd)]).astype(F32)
    inv_freq = ROPE_THETA ** (-jnp.arange(half, dtype=F32) / half)
    ang = pos[:, None] * inv_freq[None, :]
    cos, sin = jnp.cos(ang), jnp.sin(ang)
    zero = jnp.zeros_like(cos)
    rope_tabs = (jnp.concatenate([cos, cos, zero, zero], axis=1),
                 jnp.concatenate([-sin, zero, zero, zero], axis=1),
                 jnp.concatenate([zero, sin, zero, zero], axis=1))

    outs = {k: [] for k in ("p_fk", "p_fv", "p_flf", "p_hs", "p_ckv", "p_kpe", "p_mk", "p_mv",
                            "s_fk", "s_fv", "s_flf", "s_hs", "s_ckv", "s_kpe")}
    for layer in range(depth):
        if layer % 2 == 0:
            e = layer // 2
            w = ev_w_in[e]
            o_fq, o_fk, o_fv, o_fz, o_hq = 0, W8, W8 + G * d, W8 + 2 * G * d, W8 + 2 * G * d + H
            w_re = jnp.concatenate([w[:, o_hq:], w[:, o_fq:o_fz], _pad_cols(w[:, o_fz:o_hq], LANES)], axis=1)
            c_fq, c_fk, c_fv, c_fz = 4 * W8, 5 * W8, 5 * W8 + G * d, 5 * W8 + 2 * G * d
            n_cols = c_fz + LANES
            n_pad = -(-n_cols // 768) * 768
            z = norm_mm(x, ev_norm[e], _pad_cols(w_re, n_pad), name="even_in")
            bias = jnp.pad(ev_fox_bf[e], (0, LANES - H))
            zf = z[:, c_fz:c_fz + LANES]
            lf_p, cum_p = fox_gates(zf[:NP], bias, S, name="fox_gates_prompt")
            lf_s, cum_s = fox_gates(zf[NP:], bias, T, name="fox_gates_sample")
            lb = jnp.cumsum(jax.nn.softmax(ev_hg_gamma.astype(F32), axis=0), axis=0)[e]
            cq = cum_p[:, :H].reshape(B, S, H).transpose(0, 2, 1).reshape(B * H, 1, S)
            fo_p = flash_causal(z, lambda h: c_fq // d + h, d,
                                [(z, lambda g: c_fk // d + g, d)], z, lambda g: c_fv // d + g, d,
                                B=B, S=S, H=H, R=H // G, scale=d ** -0.5, cum=(cq, cum_p), name="fox_prompt")
            hg_p, hs_p = hgrn2(z, (0, 1, 2, 3), lb, ev_hg_norm[e], None, row0=0, n_seq=B, T=S, nb=1, H=HGH,
                               name="hgrn_prompt")
            lft = cache_fox_lf[e].transpose(0, 2, 1)
            cum_st = jnp.pad(cum_s[:, :H].reshape(Bd, T, H).transpose(0, 2, 1), ((0, 0), (0, 0), (0, LANES - T)))
            fo_s = fox_decode(z, c_fq // W8, c_fk // (G * d), c_fv // (G * d), cum_s, cum_st,
                              cache_fox_k[e].reshape(-1, page * G, d), cache_fox_v[e].reshape(-1, page * G, d),
                              lft, page_table, row0=NP, T=T, H=H, G=G)
            hg_s, hs_s = hgrn2(z, (0, 1, 2, 3), lb, ev_hg_norm[e], state_hgrn[e], row0=NP, n_seq=Bd, T=T, nb=2,
                               H=HGH, name="hgrn_sample")
            cat = jnp.concatenate([jnp.concatenate([fo_p, fo_s], axis=0), jnp.concatenate([hg_p, hg_s], axis=0)], axis=1)
            x = mm(cat, ev_w_out[e], x, name="even_out")
            fk = z[:, c_fk:c_fk + G * d]
            fv = z[:, c_fv:c_fv + G * d]
            outs["p_fk"].append(fk[:NP].reshape(B, S, G, d))
            outs["p_fv"].append(fv[:NP].reshape(B, S, G, d))
            outs["p_flf"].append(lf_p[:, :H].reshape(B, S, H))
            outs["p_hs"].append(hs_p)
            outs["s_fk"].append(fk[NP:].reshape(Bd, T, G, d))
            outs["s_fv"].append(fv[NP:].reshape(Bd, T, G, d))
            outs["s_flf"].append(lf_s[:, :H].reshape(Bd, T, H))
            outs["s_hs"].append(hs_s)
        else:
            o = layer // 2
            scale = (nope + rope) ** -0.5
            n_in = 2 * rank + LANES
            z1 = norm_mm(x, od_norm[o], _pad_cols(od_w_in[o], n_in), name="mla_in")
            qa_n, ckv_n, kpe_r = mla_prep(z1, od_q_norm[o], od_kv_norm[o], rope_tabs, rank=rank)
            wq = od_w_qb[o].reshape(rank, Hm, nope + rope)
            wq = jnp.pad(wq, ((0, 0), (0, 0), (0, 2 * LANES - nope - rope))).reshape(rank, Hm * 2 * LANES)
            q_full = mla_qproj(qa_n, wq, rope_tabs, scale=scale)
            w_kv = jnp.concatenate([od_w_uk[o].reshape(rank, Hm * nope), od_w_uv[o].reshape(rank, Hm * LANES)], axis=1)
            kv = mm(ckv_n[:NP], w_kv, out_dtype=BF16, name="mla_kv_up")
            at_p = flash_causal(q_full, lambda h: h, 2 * LANES,
                                [(kv, lambda g: g, LANES), (kpe_r, lambda g: 0, LANES)], kv, lambda g: Hm + g, LANES,
                                B=B, S=S, H=Hm, R=1, scale=1.0, name="mla_prompt")
            q_cat = mla_qlat(q_full, od_w_uk[o].transpose(1, 2, 0), row0=NP, rows=NS, rank=rank)
            o_lat = mla_decode(q_cat, ckv_n, kpe_r, cache_mla_ckv[o], cache_mla_kpe[o].transpose(0, 2, 1), page_table,
                               row0=NP, T=T, rope=rope)
            at_s = mla_uv(o_lat, od_w_uv[o].reshape(rank, Hm * LANES))
            x = mm(jnp.concatenate([at_p, at_s], axis=0), od_w_out[o], x, name="mla_out")
            outs["p_ckv"].append(ckv_n[:NP].reshape(B, S, rank))
            outs["p_kpe"].append(kpe_r[:NP, :rope].reshape(B, S, rope))
            outs["s_ckv"].append(ckv_n[NP:].reshape(Bd, T, rank))
            outs["s_kpe"].append(kpe_r[NP:, :rope].reshape(Bd, T, rope))
        w_mkv = jnp.concatenate([xa_wk[layer], xa_wv[layer]], axis=1)
        mkv = norm_mm(mem_prompt.reshape(B * mem_len, D), xa_mem_norm[layer], w_mkv, name="mem_kv")
        mk, mv = mkv[:, :XW], mkv[:, XW:]
        outs["p_mk"].append(mk.reshape(B, mem_len, Hx, XW // Hx))
        outs["p_mv"].append(mv.reshape(B, mem_len, Hx, XW // Hx))
        xq = norm_mm(x, xa_norm[layer], xa_wq[layer], name="mem_q")
        xo_p = mem_attend(xq, mk.reshape(B, mem_len, XW), mv.reshape(B, mem_len, XW), row0=0, rows_per_seq=S,
                          nb=1, r=512, Hm=Hx, head_rows=False, name="mem_attn_prompt")
        xo_s = mem_attend(xq, cache_mem_k[layer].reshape(Bd, mem_len * Hx, XW // Hx),
                          cache_mem_v[layer].reshape(Bd, mem_len * Hx, XW // Hx),
                          row0=NP, rows_per_seq=T, nb=8, r=T, Hm=Hx, head_rows=True, name="mem_attn_sample")
        x = mm(jnp.concatenate([xo_p, xo_s], axis=0), xa_wo[layer], x, name="mem_out")
        hmid = norm_mm(x, ff_norm[layer], ff_w1[layer], act="relu2", out_dtype=BF16, name="ff_up")
        x = mm(hmid, ff_w2[layer], x, name="ff_down")
    y = rmsnorm_rows(x, final_norm, name="final_norm")
    st = lambda k: jnp.stack(outs[k])
    return (y[:NP].reshape(B, S, D), y[NP:].reshape(Bd, T, D),
            st("p_fk"), st("p_fv"), st("p_flf"), st("p_hs"), st("p_ckv"), st("p_kpe"), st("p_mk"), st("p_mv"),
            st("s_fk"), st("s_fv"), st("s_flf"), st("s_hs"), st("s_ckv"), st("s_kpe"))
```

```python
import functools

import jax
import jax.numpy as jnp
from jax import lax
from jax.experimental import pallas as pl
from jax.experimental.pallas import tpu as pltpu

F32 = jnp.float32
BF16 = jnp.bfloat16
EPS = 1e-6
ROPE_THETA = 10000.0
LANES = 128
VMEM_LIMIT_BYTES = 56 * 1024 * 1024
HG_CHUNK = 64
FOX_PAGES_PER_STEP = 32
MLA_PAGES_PER_STEP = 32
NEG_INF = float("-inf")


def _params(*sem):
    return pltpu.CompilerParams(dimension_semantics=sem, vmem_limit_bytes=VMEM_LIMIT_BYTES)


def _pick(n, candidates):
    for c in candidates:
        if n % c == 0:
            return c
    raise ValueError(f"no tile for {n} in {candidates}")


def _dot(a, b):
    return jnp.dot(a, b, preferred_element_type=F32)


def _dot_nt(a, b):
    return lax.dot_general(a, b, (((1,), (1,)), ((), ())), preferred_element_type=F32)


def _dot_tn(a, b):
    return lax.dot_general(a, b, (((0,), (0,)), ((), ())), preferred_element_type=F32)


def _split3_dot(mask_bf16, x):
    x1 = x.astype(BF16)
    r1 = x - x1.astype(F32)
    x2 = r1.astype(BF16)
    x3 = (r1 - x2.astype(F32)).astype(BF16)
    return _dot(mask_bf16, x1) + _dot(mask_bf16, x2) + _dot(mask_bf16, x3)


def _norm_mm_kernel(x_ref, g_ref, w_ref, o_ref, xn_ref, *, act):
    @pl.when(pl.program_id(1) == 0)
    def _():
        x = x_ref[...]
        inv = lax.rsqrt(jnp.mean(x * x, axis=-1, keepdims=True) + EPS)
        xn_ref[...] = (x * inv * g_ref[...]).astype(BF16)

    acc = _dot(xn_ref[...], w_ref[...].astype(BF16))
    if act == "relu2":
        acc = jnp.maximum(acc, 0.0)
        acc = acc * acc
    o_ref[...] = acc.astype(o_ref.dtype)


def _w_spec(layer, bk, bn, imap):
    if layer is None:
        return pl.BlockSpec((bk, bn), imap)
    return pl.BlockSpec((None, bk, bn), lambda *idx: (layer,) + imap(*idx))


def norm_mm(x, g, w, *, layer=None, act=None, out_dtype=F32, name="norm_mm"):
    M, K = x.shape
    N = w.shape[-1]
    bm = _pick(M, (1024, 512, 256))
    bn = _pick(N, (1024, 768, 512, 384, 256, 128))
    return pl.pallas_call(
        functools.partial(_norm_mm_kernel, act=act),
        grid=(M // bm, N // bn),
        in_specs=[
            pl.BlockSpec((bm, K), lambda i, j: (i, 0)),
            pl.BlockSpec((1, K), lambda i, j: (0, 0)),
            _w_spec(layer, K, bn, lambda i, j: (0, j)),
        ],
        out_specs=pl.BlockSpec((bm, bn), lambda i, j: (i, j)),
        out_shape=jax.ShapeDtypeStruct((M, N), out_dtype),
        scratch_shapes=[pltpu.VMEM((bm, K), BF16)],
        compiler_params=_params("parallel", "arbitrary"),
        name=name,
    )(x, g.reshape(1, K).astype(F32), w)


def _mm_kernel(*refs, has_res, nk):
    a_ref, w_ref = refs[0], refs[1]
    r_ref = refs[2] if has_res else None
    o_ref = refs[3] if has_res else refs[2]
    part = _dot(a_ref[...].astype(BF16), w_ref[...].astype(BF16))
    if nk == 1:
        if has_res:
            part = part + r_ref[...]
        o_ref[...] = part.astype(o_ref.dtype)
        return
    acc_ref = refs[-1]
    k = pl.program_id(2)

    @pl.when(k == 0)
    def _():
        acc_ref[...] = part

    @pl.when(k > 0)
    def _():
        acc_ref[...] += part

    @pl.when(k == nk - 1)
    def _():
        out = acc_ref[...]
        if has_res:
            out = out + r_ref[...]
        o_ref[...] = out.astype(o_ref.dtype)


def mm(a, w, res=None, *, layer=None, out_dtype=F32, name="mm"):
    M, K = a.shape
    N = w.shape[-1]
    bm = _pick(M, (1024, 512, 256))
    bn = _pick(N, (1024, 512, 384, 256, 128))
    bk = K if K <= 2048 else 2048
    nk = K // bk
    in_specs = [
        pl.BlockSpec((bm, bk), lambda i, j, k: (i, k)),
        _w_spec(layer, bk, bn, lambda i, j, k: (k, j)),
    ]
    args = [a, w]
    if res is not None:
        in_specs.append(pl.BlockSpec((bm, bn), lambda i, j, k: (i, j)))
        args.append(res)
    return pl.pallas_call(
        functools.partial(_mm_kernel, has_res=res is not None, nk=nk),
        grid=(M // bm, N // bn, nk),
        in_specs=in_specs,
        out_specs=pl.BlockSpec((bm, bn), lambda i, j, k: (i, j)),
        out_shape=jax.ShapeDtypeStruct((M, N), out_dtype),
        scratch_shapes=[pltpu.VMEM((bm, bn), F32)] if nk > 1 else [],
        compiler_params=_params("parallel", "parallel", "arbitrary"),
        name=name,
    )(*args)


def _rmsnorm_kernel(x_ref, g_ref, o_ref):
    x = x_ref[...]
    inv = lax.rsqrt(jnp.mean(x * x, axis=-1, keepdims=True) + EPS)
    o_ref[...] = x * inv * g_ref[...]


def rmsnorm_rows(x, g, *, row0, rows, name="rmsnorm"):
    K = x.shape[1]
    bm = _pick(rows, (512, 256))
    assert row0 % bm == 0
    r0 = row0 // bm
    return pl.pallas_call(
        _rmsnorm_kernel,
        grid=(rows // bm,),
        in_specs=[pl.BlockSpec((bm, K), lambda i: (r0 + i, 0)), pl.BlockSpec((1, K), lambda i: (0, 0))],
        out_specs=pl.BlockSpec((bm, K), lambda i: (i, 0)),
        out_shape=jax.ShapeDtypeStruct((rows, K), F32),
        compiler_params=_params("parallel"),
        name=name,
    )(x, g.reshape(1, K).astype(F32))


def _log_sigmoid(x):
    return jnp.minimum(x, 0.0) - jnp.log(1.0 + jnp.exp(-jnp.abs(x)))


def _gate_kernel(z_ref, b_ref, lf_ref, cum_ref, carry_ref, *, br, seg):
    lf = _log_sigmoid(z_ref[...] + b_ref[...])
    lf_ref[...] = lf
    row = lax.broadcasted_iota(jnp.int32, (br, br), 0)
    col = lax.broadcasted_iota(jnp.int32, (br, br), 1)
    if seg >= br:
        mask = col <= row
    else:
        mask = (col <= row) & ((row // seg) == (col // seg))
    cum = _split3_dot(jnp.where(mask, 1.0, 0.0).astype(BF16), lf)
    if seg > br:
        blocks_per_seg = seg // br

        @pl.when(pl.program_id(0) % blocks_per_seg == 0)
        def _():
            carry_ref[...] = jnp.zeros_like(carry_ref)

        cum = cum + carry_ref[...]
        carry_ref[...] = cum[br - 1:br, :]
    cum_ref[...] = cum


def fox_gates(zf, bias, seg, name="fox_gates"):
    M = zf.shape[0]
    br = _pick(M, (512, 256, 128))
    assert seg % br == 0 or br % seg == 0
    return pl.pallas_call(
        functools.partial(_gate_kernel, br=br, seg=seg),
        grid=(M // br,),
        in_specs=[pl.BlockSpec((br, LANES), lambda i: (i, 0)), pl.BlockSpec((1, LANES), lambda i: (0, 0))],
        out_specs=[pl.BlockSpec((br, LANES), lambda i: (i, 0)), pl.BlockSpec((br, LANES), lambda i: (i, 0))],
        out_shape=[jax.ShapeDtypeStruct((M, LANES), F32), jax.ShapeDtypeStruct((M, LANES), F32)],
        scratch_shapes=[pltpu.VMEM((1, LANES), F32)],
        compiler_params=_params("arbitrary"),
        name=name,
    )(zf, bias.reshape(1, LANES))


def _flash_kernel(*refs, n_kparts, has_cum, bq, bk, scale, nk, n_sub):
    q_ref = refs[0]
    k_refs = refs[1:1 + n_kparts]
    v_ref = refs[1 + n_kparts]
    pos = 2 + n_kparts
    if has_cum:
        cq_ref, ck_ref = refs[pos], refs[pos + 1]
        pos += 2
    o_ref = refs[pos]
    m_ref, l_ref, acc_ref, qt_s = refs[pos + 1:pos + 5]
    h = pl.program_id(1)
    i = pl.program_id(2)
    j = pl.program_id(3)
    hw = bq // n_sub
    last_j = (i + 1) * (bq // bk) - 1
    first_diag_j = i * (bq // bk)

    @pl.when(j == 0)
    def _():
        m_ref[...] = jnp.full_like(m_ref, NEG_INF)
        l_ref[...] = jnp.zeros_like(l_ref)
        acc_ref[...] = jnp.zeros_like(acc_ref)
        qt_s[...] = (q_ref[...].astype(F32) * scale).T.astype(BF16)

    def step(masked):
        if n_kparts == 1:
            k = k_refs[0][...].astype(BF16)
        else:
            k = jnp.concatenate([kr[...].astype(BF16) for kr in k_refs], axis=-1)
        v = v_ref[...].astype(BF16)
        if has_cum:
            sel = lax.broadcasted_iota(jnp.int32, (LANES, LANES), 0) == h
            ck_rep = _split3_dot_right(ck_ref[...], jnp.where(sel, 1.0, 0.0).astype(BF16))
            ck_rep = jnp.concatenate([ck_rep] * (hw // LANES), axis=1)
        m_old = m_ref[...]
        sts = []
        for sub in range(n_sub):
            cols = slice(sub * hw, (sub + 1) * hw)
            st = _dot(k, qt_s[:, cols])
            if has_cum:
                st = st + cq_ref[0][:, cols] - ck_rep
            if masked:
                key = lax.broadcasted_iota(jnp.int32, (bk, hw), 0) + (j * bk - i * bq)
                qry = lax.broadcasted_iota(jnp.int32, (bk, hw), 1) + sub * hw
                st = jnp.where(key <= qry, st, NEG_INF)
            sts.append(st)
        m_new = jnp.maximum(m_old, jnp.concatenate([jnp.max(st, axis=0, keepdims=True) for st in sts], axis=1))
        alpha = jnp.exp(m_old - m_new)
        pts = [jnp.exp(st - m_new[:, sub * hw:(sub + 1) * hw]) for sub, st in enumerate(sts)]
        l_ref[...] = l_ref[...] * alpha + jnp.concatenate([jnp.sum(pt, axis=0, keepdims=True) for pt in pts], axis=1)
        pv = jnp.concatenate([_dot_tn(v, pt.astype(BF16)) for pt in pts], axis=1)
        acc_ref[...] = acc_ref[...] * alpha + pv
        m_ref[...] = m_new

    @pl.when(j < first_diag_j)
    def _():
        step(False)

    @pl.when((j >= first_diag_j) & (j <= last_j))
    def _():
        step(True)

    @pl.when(j == nk - 1)
    def _():
        o_ref[...] = (acc_ref[...] / l_ref[...]).T.astype(o_ref.dtype)


def flash_causal(q_arr, q_col, dq, k_parts, v_arr, v_col, dv, *, B, S, H, R, scale, cum=None, name="flash"):
    bq, bk, n_sub = 1024, 512, 4
    nq, nk = S // bq, S // bk
    ratio = bq // bk

    def kv_row(b, i, j):
        return b * nk + jnp.minimum(j, (i + 1) * ratio - 1)

    in_specs = [pl.BlockSpec((bq, dq), lambda b, h, i, j: (b * nq + i, q_col(h)))]
    args = [q_arr]
    for arr, col, width in k_parts:
        in_specs.append(pl.BlockSpec((bk, width), lambda b, h, i, j, col=col: (kv_row(b, i, j), col(h // R))))
        args.append(arr)
    in_specs.append(pl.BlockSpec((bk, dv), lambda b, h, i, j: (kv_row(b, i, j), v_col(h // R))))
    args.append(v_arr)
    scratch = [pltpu.VMEM((1, bq), F32), pltpu.VMEM((1, bq), F32), pltpu.VMEM((dv, bq), F32),
               pltpu.VMEM((dq, bq), BF16)]
    if cum is not None:
        cq, ck = cum
        in_specs.append(pl.BlockSpec((1, 1, bq), lambda b, h, i, j: (b * H + h, 0, i)))
        in_specs.append(pl.BlockSpec((bk, LANES), lambda b, h, i, j: (kv_row(b, i, j), 0)))
        args += [cq, ck]
    return pl.pallas_call(
        functools.partial(_flash_kernel, n_kparts=len(k_parts), has_cum=cum is not None, bq=bq, bk=bk, scale=scale,
                          nk=nk, n_sub=n_sub),
        grid=(B, H, nq, nk),
        in_specs=in_specs,
        out_specs=pl.BlockSpec((bq, dv), lambda b, h, i, j: (b * nq + i, h)),
        out_shape=jax.ShapeDtypeStruct((B * S, H * dv), BF16),
        scratch_shapes=scratch,
        compiler_params=_params("parallel", "parallel", "parallel", "arbitrary"),
        name=name,
    )(*args)


def _cumsum_rows(x):
    c = x.shape[0]
    row = lax.broadcasted_iota(jnp.int32, x.shape, 0)
    s = 1
    while s < c:
        x = x + jnp.where(row >= s, pltpu.roll(x, s, 0), 0.0)
        s *= 2
    return x


def _hgrn_kernel(*refs, c, nb, H, has_init, n_chunks):
    q_ref, f_ref, i_ref, g_ref, lb_ref, nw_ref = refs[:6]
    pos = 6
    s0_ref = None
    if has_init:
        s0_ref = refs[pos]
        pos += 1
    o_ref, sout_ref, st_ref = refs[pos], refs[pos + 1], refs[pos + 2]
    ci = pl.program_id(1)
    d = LANES

    @pl.when(ci == 0)
    def _():
        if has_init:
            for n in range(nb):
                for h in range(H):
                    st_ref[n, h] = s0_ref[n, h].T
        else:
            st_ref[...] = jnp.zeros_like(st_ref)

    row = lax.broadcasted_iota(jnp.int32, (c, c), 0)
    col = lax.broadcasted_iota(jnp.int32, (c, c), 1)
    causal = col <= row
    nw = nw_ref[...]
    for n in range(nb):
        rows = slice(n * c, (n + 1) * c)
        for h in range(H):
            cols = slice(h * d, (h + 1) * d)
            lb = lb_ref[:, cols]
            f = lb + (1.0 - lb) * jax.nn.sigmoid(f_ref[rows, cols])
            kk = 1.0 - f
            b = _cumsum_rows(jnp.log(f))
            b_last = b[c - 1:c, :]
            q_e = (q_ref[rows, cols] * jnp.exp(b)).astype(BF16)
            k_e = (kk * jnp.exp(-b)).astype(BF16)
            v = i_ref[rows, cols].astype(BF16)
            att = jnp.where(causal, _dot_nt(q_e, k_e), 0.0).astype(BF16)
            st = st_ref[n, h]
            o = _dot(att, v) + _dot_nt(q_e, st.astype(BF16))
            kd = (kk * jnp.exp(b_last - b)).astype(BF16)
            st_ref[n, h] = st * jnp.exp(b_last) + _dot_tn(v, kd)
            on = o * lax.rsqrt(jnp.mean(o * o, axis=-1, keepdims=True) + EPS) * nw
            gate = g_ref[rows, cols]
            o_ref[rows, cols] = (on * (gate * jax.nn.sigmoid(gate))).astype(o_ref.dtype)

    @pl.when(ci == n_chunks - 1)
    def _():
        for n in range(nb):
            for h in range(H):
                sout_ref[n, h] = st_ref[n, h].T


def hgrn2(z, cols, lb, nw, s0, *, row0, n_seq, T, nb, H, name="hgrn2"):
    c = min(HG_CHUNK, T)
    n_chunks = T // c
    W = H * LANES
    rb = nb * c
    assert (n_chunks == 1 or nb == 1) and row0 % rb == 0 and n_seq % nb == 0
    r0 = row0 // rb

    def zspec(cb):
        return pl.BlockSpec((rb, W), lambda s, ci, cb=cb: (r0 + s * n_chunks + ci, cb))

    in_specs = [zspec(cb) for cb in cols] + [
        pl.BlockSpec((1, W), lambda s, ci: (0, 0)),
        pl.BlockSpec((1, LANES), lambda s, ci: (0, 0)),
    ]
    args = [z, z, z, z, lb.reshape(1, W), nw.reshape(1, LANES)]
    if s0 is not None:
        in_specs.append(pl.BlockSpec((nb, H, LANES, LANES), lambda s, ci: (s, 0, 0, 0)))
        args.append(s0)
    return pl.pallas_call(
        functools.partial(_hgrn_kernel, c=c, nb=nb, H=H, has_init=s0 is not None, n_chunks=n_chunks),
        grid=(n_seq // nb, n_chunks),
        in_specs=in_specs,
        out_specs=[
            pl.BlockSpec((rb, W), lambda s, ci: (s * n_chunks + ci, 0)),
            pl.BlockSpec((nb, H, LANES, LANES), lambda s, ci: (s, 0, 0, 0)),
        ],
        out_shape=[jax.ShapeDtypeStruct((n_seq * T, W), BF16), jax.ShapeDtypeStruct((n_seq, H, LANES, LANES), F32)],
        scratch_shapes=[pltpu.VMEM((nb, H, LANES, LANES), F32)],
        compiler_params=_params("parallel", "arbitrary"),
        name=name,
    )(*args)


def _page_copies(pt_ref, pools, bufs, sems, b, chunk, slot, P):
    out = []
    for jj in range(P):
        pg = pt_ref[b, chunk * P + jj]
        for a, (pool, buf) in enumerate(zip(pools, bufs)):
            out.append(pltpu.make_async_copy(pool.at[pg], buf.at[slot, jj], sems.at[a, slot]))
    return out


def _gather_pages(pt_ref, pools, bufs, sems, *, P, n_steps, n_seq, chunk_of_step):
    b = pl.program_id(0)
    step = pl.program_id(1)
    g = b * n_steps + step
    slot = lax.rem(g, 2)

    @pl.when(g == 0)
    def _():
        for c in _page_copies(pt_ref, pools, bufs, sems, 0, chunk_of_step(0), 0, P):
            c.start()

    @pl.when(g + 1 < n_seq * n_steps)
    def _():
        nxt = g + 1
        nb = nxt // n_steps
        for c in _page_copies(pt_ref, pools, bufs, sems, nb, chunk_of_step(nxt - nb * n_steps), 1 - slot, P):
            c.start()

    for c in _page_copies(pt_ref, pools, bufs, sems, b, chunk_of_step(step), slot, P):
        c.wait()
    return slot


def _fox_decode_kernel(pt_ref, q_ref, kn_ref, vn_ref, cn_ref, cnt_ref, k_hbm, v_hbm, lf_hbm, o_ref,
                       kbuf, vbuf, lfbuf, sems, qbd_s, m_s, l_s, acc_s, suf_s, *, P, T, H, G, scale, n_steps, n_seq):
    d = LANES
    R = H // G
    HT = H * T
    step = pl.program_id(1)
    slot = _gather_pages(pt_ref, (k_hbm, v_hbm, lf_hbm), (kbuf, vbuf, lfbuf), sems, P=P, n_steps=n_steps,
                         n_seq=n_seq, chunk_of_step=lambda s: n_steps - 1 - s)

    def page_rows(buf, jj):
        ref = buf.at[slot, jj]
        return jnp.concatenate([ref[pl.ds(g, LANES, stride=G), :] for g in range(G)], axis=1).astype(BF16)

    def head_rows(x):
        return jnp.concatenate([jnp.broadcast_to(x[h:h + 1, :], (T, x.shape[1])) for h in range(H)], axis=0)

    @pl.when(step == 0)
    def _():
        q = q_ref[...] * scale
        zero = jnp.zeros((T, d), F32)
        blocks = []
        for h in range(H):
            g = h // R
            blocks.append(jnp.concatenate([q[:, h * d:(h + 1) * d] if gg == g else zero for gg in range(G)], axis=1))
        qbd = jnp.concatenate(blocks, axis=0).astype(BF16)
        qbd_s[...] = qbd
        pad = jnp.zeros((LANES - T, G * d), F32)
        kn = jnp.concatenate([kn_ref[...], pad], axis=0).astype(BF16)
        vn = jnp.concatenate([vn_ref[...], pad], axis=0).astype(BF16)
        cn = cn_ref[...]
        cn_rows = jnp.concatenate([cn[:, h:h + 1] for h in range(H)], axis=0)
        cn_cols = head_rows(cnt_ref[...])
        s = _dot_nt(qbd, kn) + cn_rows - cn_cols
        t_idx = lax.broadcasted_iota(jnp.int32, (HT, LANES), 0) % T
        s_idx = lax.broadcasted_iota(jnp.int32, (HT, LANES), 1)
        s = jnp.where(s_idx <= t_idx, s, NEG_INF)
        m = jnp.max(s, axis=-1, keepdims=True)
        p = jnp.exp(s - m)
        m_s[...] = m
        l_s[...] = jnp.sum(p, axis=-1, keepdims=True)
        acc_s[...] = _dot(p.astype(BF16), vn)
        suf_s[...] = cn_rows

    tri = (lax.broadcasted_iota(jnp.int32, (LANES, LANES), 0) <= lax.broadcasted_iota(jnp.int32, (LANES, LANES), 1))
    tri = jnp.where(tri, 1.0, 0.0).astype(BF16)
    qbd = qbd_s[...]
    suf = suf_s[...]
    cw_all = _split3_dot_right(lfbuf[slot].reshape(P * H, LANES), tri)
    s_parts = [None] * P
    for jj in range(P - 1, -1, -1):
        cw_rows = head_rows(cw_all[jj * H:(jj + 1) * H, :])
        tot = cw_rows[:, LANES - 1:LANES]
        s = _dot_nt(qbd, page_rows(kbuf, jj))
        s_parts[jj] = s + (suf + tot) - cw_rows
        suf = suf + tot
    suf_s[...] = suf
    m_old = m_s[...]
    m_new = m_old
    for jj in range(P):
        m_new = jnp.maximum(m_new, jnp.max(s_parts[jj], axis=-1, keepdims=True))
    alpha = jnp.exp(m_old - m_new)
    l_new = l_s[...] * alpha
    acc = acc_s[...] * alpha
    for jj in range(P):
        p = jnp.exp(s_parts[jj] - m_new)
        l_new = l_new + jnp.sum(p, axis=-1, keepdims=True)
        acc = acc + _dot(p.astype(BF16), page_rows(vbuf, jj))
    m_s[...] = m_new
    l_s[...] = l_new
    acc_s[...] = acc

    @pl.when(step == n_steps - 1)
    def _():
        out = acc / l_new
        o_ref[...] = jnp.concatenate(
            [out[h * T:(h + 1) * T, (h // R) * d:(h // R + 1) * d] for h in range(H)], axis=1).astype(o_ref.dtype)


def _split3_dot_right(x, mask_bf16):
    x1 = x.astype(BF16)
    r1 = x - x1.astype(F32)
    x2 = r1.astype(BF16)
    x3 = (r1 - x2.astype(F32)).astype(BF16)
    return _dot(x1, mask_bf16) + _dot(x2, mask_bf16) + _dot(x3, mask_bf16)


def fox_decode(z, q_cb, k_cb, v_cb, cnew, cnew_t, pool_k, pool_v, pool_lft, page_table, *, row0, T, H, G, name="fox_decode"):
    Bd, n_pages = page_table.shape
    P = FOX_PAGES_PER_STEP
    n_steps = n_pages // P
    d = LANES
    page = pool_lft.shape[2]
    assert page == LANES and pool_k.shape[1] == page * G and n_pages % P == 0 and row0 % T == 0
    r0 = row0 // T
    hbm = pl.BlockSpec(memory_space=pl.ANY)
    in_specs = [
        pl.BlockSpec((T, H * d), lambda b, s, pt: (r0 + b, q_cb)),
        pl.BlockSpec((T, G * d), lambda b, s, pt: (r0 + b, k_cb)),
        pl.BlockSpec((T, G * d), lambda b, s, pt: (r0 + b, v_cb)),
        pl.BlockSpec((T, LANES), lambda b, s, pt: (b, 0)),
        pl.BlockSpec((None, H, LANES), lambda b, s, pt: (b, 0, 0)),
        hbm, hbm, hbm,
    ]
    HT = H * T
    grid_spec = pltpu.PrefetchScalarGridSpec(
        num_scalar_prefetch=1,
        grid=(Bd, n_steps),
        in_specs=in_specs,
        out_specs=pl.BlockSpec((T, H * d), lambda b, s, pt: (b, 0)),
        scratch_shapes=[pltpu.VMEM((2, P, page * G, d), F32), pltpu.VMEM((2, P, page * G, d), F32),
                        pltpu.VMEM((2, P, H, page), F32), pltpu.SemaphoreType.DMA((3, 2)),
                        pltpu.VMEM((HT, G * d), BF16), pltpu.VMEM((HT, 1), F32), pltpu.VMEM((HT, 1), F32),
                        pltpu.VMEM((HT, G * d), F32), pltpu.VMEM((HT, 1), F32)],
    )
    return pl.pallas_call(
        functools.partial(_fox_decode_kernel, P=P, T=T, H=H, G=G, scale=d ** -0.5, n_steps=n_steps, n_seq=Bd),
        grid_spec=grid_spec,
        out_shape=jax.ShapeDtypeStruct((Bd * T, H * d), BF16),
        compiler_params=_params("arbitrary", "arbitrary"),
        name=name,
    )(page_table, z, z, z, cnew, cnew_t, pool_k, pool_v, pool_lft)


def _rope_block(v, c, s1, s2):
    return v * c + pltpu.roll(v, 96, 1) * s1 + pltpu.roll(v, 32, 1) * s2


def _mla_prep_kernel(z_ref, qg_ref, kg_ref, c_ref, s1_ref, s2_ref, qa_ref, ckv_ref, kpe_ref, *, rank):
    qa = z_ref[:, :rank]
    inv = lax.rsqrt(jnp.mean(qa * qa, axis=-1, keepdims=True) + EPS)
    qa_ref[...] = (qa * inv * qg_ref[...]).astype(qa_ref.dtype)
    ckv = z_ref[:, rank:2 * rank]
    inv = lax.rsqrt(jnp.mean(ckv * ckv, axis=-1, keepdims=True) + EPS)
    ckv_ref[...] = ckv * inv * kg_ref[...]
    kpe_ref[...] = _rope_block(z_ref[:, 2 * rank:2 * rank + LANES], c_ref[...], s1_ref[...], s2_ref[...])


def mla_prep(z1, q_norm, kv_norm, rope_tabs, *, rank, name="mla_prep"):
    M, W = z1.shape
    bm = _pick(M, (1024, 512, 256))
    row = lambda i: (i, 0)
    fix = lambda i: (0, 0)
    return pl.pallas_call(
        functools.partial(_mla_prep_kernel, rank=rank),
        grid=(M // bm,),
        in_specs=[pl.BlockSpec((bm, W), row), pl.BlockSpec((1, rank), fix), pl.BlockSpec((1, rank), fix)]
        + [pl.BlockSpec((bm, LANES), row)] * 3,
        out_specs=[pl.BlockSpec((bm, rank), row), pl.BlockSpec((bm, rank), row), pl.BlockSpec((bm, LANES), row)],
        out_shape=[jax.ShapeDtypeStruct((M, rank), BF16), jax.ShapeDtypeStruct((M, rank), F32),
                   jax.ShapeDtypeStruct((M, LANES), F32)],
        compiler_params=_params("parallel"),
        name=name,
    )(z1, q_norm.reshape(1, rank), kv_norm.reshape(1, rank), *rope_tabs)


def _qproj_kernel(a_ref, w_ref, c_ref, s1_ref, s2_ref, o_ref, *, heads_per_block, scale):
    acc = _dot(a_ref[...], w_ref[...].astype(BF16)) * scale
    c, s1, s2 = c_ref[...], s1_ref[...], s2_ref[...]
    for hh in range(heads_per_block):
        base = hh * 2 * LANES
        o_ref[:, base:base + LANES] = acc[:, base:base + LANES].astype(o_ref.dtype)
        o_ref[:, base + LANES:base + 2 * LANES] = _rope_block(
            acc[:, base + LANES:base + 2 * LANES], c, s1, s2).astype(o_ref.dtype)


def mla_qproj(qa_n, w_qb_pad, rope_tabs, *, scale, name="mla_qproj"):
    M, K = qa_n.shape
    N = w_qb_pad.shape[1]
    bm = _pick(M, (1024, 512, 256))
    bn = 1024
    row = lambda i, j: (i, 0)
    return pl.pallas_call(
        functools.partial(_qproj_kernel, heads_per_block=bn // (2 * LANES), scale=scale),
        grid=(M // bm, N // bn),
        in_specs=[pl.BlockSpec((bm, K), row), pl.BlockSpec((K, bn), lambda i, j: (0, j))]
        + [pl.BlockSpec((bm, LANES), row)] * 3,
        out_specs=pl.BlockSpec((bm, bn), lambda i, j: (i, j)),
        out_shape=jax.ShapeDtypeStruct((M, N), BF16),
        compiler_params=_params("parallel", "parallel"),
        name=name,
    )(qa_n, w_qb_pad, *rope_tabs)


def _qlat_kernel(q_ref, w_ref, o_ref, *, rank):
    q = q_ref[...]
    o_ref[0, :, :rank] = _dot(q[:, :LANES], w_ref[0].astype(BF16))
    o_ref[0, :, rank:] = q[:, LANES:].astype(F32)


def mla_qlat(q_full, w_ukt, *, row0, rows, rank, name="mla_qlat"):
    Hh = w_ukt.shape[0]
    r0 = row0 // rows
    assert row0 % rows == 0
    return pl.pallas_call(
        functools.partial(_qlat_kernel, rank=rank),
        grid=(Hh,),
        in_specs=[pl.BlockSpec((rows, 2 * LANES), lambda h: (r0, h)),
                  pl.BlockSpec((1, LANES, rank), lambda h: (h, 0, 0))],
        out_specs=pl.BlockSpec((1, rows, rank + LANES), lambda h: (h, 0, 0)),
        out_shape=jax.ShapeDtypeStruct((Hh, rows, rank + LANES), F32),
        compiler_params=_params("parallel"),
        name=name,
    )(q_full, w_ukt)


def _uv_kernel(x_ref, w_ref, o_ref):
    o_ref[...] = _dot(x_ref[0].astype(BF16), w_ref[...].astype(BF16)).astype(o_ref.dtype)


def mla_uv(o_lat, w_uv2d, name="mla_uv"):
    Hh, rows, rank = o_lat.shape
    return pl.pallas_call(
        _uv_kernel,
        grid=(Hh,),
        in_specs=[pl.BlockSpec((1, rows, rank), lambda h: (h, 0, 0)), pl.BlockSpec((rank, LANES), lambda h: (0, h))],
        out_specs=pl.BlockSpec((rows, LANES), lambda h: (0, h)),
        out_shape=jax.ShapeDtypeStruct((rows, Hh * LANES), BF16),
        compiler_params=_params("parallel"),
        name=name,
    )(o_lat, w_uv2d)


def _mla_decode_kernel(pt_ref, q_ref, cn_ref, kn_ref, c_hbm, r_hbm, o_ref, cbuf, rbuf, sems,
                       ql_s, qr_s, m_s, l_s, acc_s, *, P, T, Hh, rank, rope, n_steps, n_seq):
    HT = Hh * T
    step = pl.program_id(1)
    slot = _gather_pages(pt_ref, (c_hbm, r_hbm), (cbuf, rbuf), sems, P=P, n_steps=n_steps, n_seq=n_seq,
                         chunk_of_step=lambda s: s)

    @pl.when(step == 0)
    def _():
        q = q_ref[...].reshape(HT, rank + LANES)
        ql = q[:, :rank].astype(BF16)
        qr = q[:, rank:rank + rope].astype(BF16)
        ql_s[...] = ql
        qr_s[...] = qr
        cn = jnp.concatenate([cn_ref[...], jnp.zeros((LANES - T, rank), F32)], axis=0).astype(BF16)
        kn = jnp.concatenate([kn_ref[:, :rope], jnp.zeros((LANES - T, rope), F32)], axis=0).astype(BF16)
        s = _dot_nt(ql, cn) + _dot_nt(qr, kn)
        t_idx = lax.broadcasted_iota(jnp.int32, (HT, LANES), 0) % T
        s_idx = lax.broadcasted_iota(jnp.int32, (HT, LANES), 1)
        s = jnp.where(s_idx <= t_idx, s, NEG_INF)
        m = jnp.max(s, axis=-1, keepdims=True)
        p = jnp.exp(s - m)
        m_s[...] = m
        l_s[...] = jnp.sum(p, axis=-1, keepdims=True)
        acc_s[...] = _dot(p.astype(BF16), cn)

    ql = ql_s[...]
    qr = qr_s[...]
    n_pairs = P // 2
    ckv = [jnp.concatenate([cbuf[slot, 2 * jj].astype(BF16), cbuf[slot, 2 * jj + 1].astype(BF16)], axis=0)
           for jj in range(n_pairs)]
    s_parts = []
    for jj in range(n_pairs):
        kpe_t = jnp.concatenate([rbuf[slot, 2 * jj], rbuf[slot, 2 * jj + 1]], axis=1).astype(BF16)
        s_parts.append(_dot_nt(ql, ckv[jj]) + _dot(qr, kpe_t))
    m_old = m_s[...]
    m_new = m_old
    for jj in range(n_pairs):
        m_new = jnp.maximum(m_new, jnp.max(s_parts[jj], axis=-1, keepdims=True))
    alpha = jnp.exp(m_old - m_new)
    l_new = l_s[...] * alpha
    acc = acc_s[...] * alpha
    for jj in range(n_pairs):
        p = jnp.exp(s_parts[jj] - m_new)
        l_new = l_new + jnp.sum(p, axis=-1, keepdims=True)
        acc = acc + _dot(p.astype(BF16), ckv[jj])
    m_s[...] = m_new
    l_s[...] = l_new
    acc_s[...] = acc

    @pl.when(step == n_steps - 1)
    def _():
        o_ref[...] = (acc / l_new).reshape(Hh, T, rank)


def mla_decode(q_cat, ckv_n, kpe_r, pool_ckv, pool_kpe_t, page_table, *, row0, T, rope, name="mla_decode"):
    Hh = q_cat.shape[0]
    rank = pool_ckv.shape[-1]
    Bd, n_pages = page_table.shape
    P = MLA_PAGES_PER_STEP
    n_steps = n_pages // P
    page = pool_ckv.shape[1]
    assert page == LANES and pool_kpe_t.shape[1:] == (rope, page) and n_pages % P == 0 and P % 2 == 0 and row0 % T == 0
    r0 = row0 // T
    hbm = pl.BlockSpec(memory_space=pl.ANY)
    in_specs = [
        pl.BlockSpec((Hh, T, rank + LANES), lambda b, s, pt: (0, b, 0)),
        pl.BlockSpec((T, rank), lambda b, s, pt: (r0 + b, 0)),
        pl.BlockSpec((T, LANES), lambda b, s, pt: (r0 + b, 0)),
        hbm, hbm,
    ]
    HT = Hh * T
    grid_spec = pltpu.PrefetchScalarGridSpec(
        num_scalar_prefetch=1,
        grid=(Bd, n_steps),
        in_specs=in_specs,
        out_specs=pl.BlockSpec((Hh, T, rank), lambda b, s, pt: (0, b, 0)),
        scratch_shapes=[pltpu.VMEM((2, P, page, rank), F32), pltpu.VMEM((2, P, rope, page), F32),
                        pltpu.SemaphoreType.DMA((2, 2)),
                        pltpu.VMEM((HT, rank), BF16), pltpu.VMEM((HT, rope), BF16), pltpu.VMEM((HT, 1), F32),
                        pltpu.VMEM((HT, 1), F32), pltpu.VMEM((HT, rank), F32)],
    )
    return pl.pallas_call(
        functools.partial(_mla_decode_kernel, P=P, T=T, Hh=Hh, rank=rank, rope=rope, n_steps=n_steps, n_seq=Bd),
        grid_spec=grid_spec,
        out_shape=jax.ShapeDtypeStruct((Hh, Bd * T, rank), F32),
        compiler_params=_params("arbitrary", "arbitrary"),
        name=name,
    )(page_table, q_cat, ckv_n, kpe_r, pool_ckv, pool_kpe_t)


def _mem_attn_kernel(q_ref, k_ref, v_ref, o_ref, *, nb, r, Hm, scale, head_rows, mem_len):
    d = LANES
    pairs = [(n, h) for n in range(nb) for h in range(Hm)]

    def mem(ref, n, h):
        if head_rows:
            return ref[n, pl.ds(h, mem_len, stride=Hm), :].astype(BF16)
        return ref[n, :, h * d:(h + 1) * d].astype(BF16)

    q_all = (q_ref[...] * scale).astype(BF16)
    s_all = [_dot_nt(q_all[n * r:(n + 1) * r, h * d:(h + 1) * d], mem(k_ref, n, h)) for n, h in pairs]
    p_all, l_all = [], []
    for s in s_all:
        p = jnp.exp(s - jnp.max(s, axis=-1, keepdims=True))
        p_all.append(p.astype(BF16))
        l_all.append(jnp.sum(p, axis=-1, keepdims=True))
    o_all = [_dot(p, mem(v_ref, n, h)) / l for (n, h), p, l in zip(pairs, p_all, l_all)]
    o_ref[...] = jnp.concatenate(
        [jnp.concatenate(o_all[n * Hm:(n + 1) * Hm], axis=1) for n in range(nb)], axis=0).astype(o_ref.dtype)


def mem_attend(q, mk, mv, *, row0, rows_per_seq, nb, r, Hm, head_rows, layer=None, name="mem_attn"):
    n_seq, M, W = mk.shape[-3:]
    mem_len = M // Hm if head_rows else M
    blocks_per_seq = rows_per_seq // r
    assert (nb == 1 or blocks_per_seq == 1) and row0 % (nb * r) == 0 and n_seq % nb == 0
    r0 = row0 // (nb * r)
    n_blocks = n_seq * blocks_per_seq // nb
    if layer is None:
        kv_spec = pl.BlockSpec((nb, M, W), lambda i: (i // blocks_per_seq, 0, 0))
    else:
        kv_spec = pl.BlockSpec((None, nb, M, W), lambda i: (layer, i // blocks_per_seq, 0, 0))
    QW = Hm * LANES
    return pl.pallas_call(
        functools.partial(_mem_attn_kernel, nb=nb, r=r, Hm=Hm, scale=LANES ** -0.5, head_rows=head_rows,
                          mem_len=mem_len),
        grid=(n_blocks,),
        in_specs=[pl.BlockSpec((nb * r, QW), lambda i: (r0 + i, 0)), kv_spec, kv_spec],
        out_specs=pl.BlockSpec((nb * r, QW), lambda i: (i, 0)),
        out_shape=jax.ShapeDtypeStruct((n_seq * rows_per_seq, QW), BF16),
        compiler_params=_params("parallel"),
        name=name,
    )(q, mk, mv)


def _pad_cols(w, n):
    return jnp.pad(w, ((0, 0), (0, n - w.shape[1])))


def kernel(x_prompt, x_sample, mem_prompt, page_table, cache_fox_k, cache_fox_v, cache_fox_lf, state_hgrn, cache_mla_ckv, cache_mla_kpe, cache_mem_k, cache_mem_v, ev_norm, ev_w_in, ev_fox_bf, ev_hg_gamma, ev_hg_norm, ev_w_out, od_norm, od_w_in, od_q_norm, od_w_qb, od_kv_norm, od_w_uk, od_w_uv, od_w_out, xa_norm, xa_mem_norm, xa_wq, xa_wk, xa_wv, xa_wo, ff_norm, ff_w1, ff_w2, final_norm):
    B, S, D = x_prompt.shape
    Bd, T, _ = x_sample.shape
    depth = ff_w1.shape[0]
    n_pages = page_table.shape[1]
    page = cache_fox_k.shape[2]
    past_len = n_pages * page
    NP, NS = B * S, Bd * T
    N = NP + NS
    G = cache_fox_k.shape[3]
    d = cache_fox_k.shape[4]
    H = cache_fox_lf.shape[3]
    HGH = state_hgrn.shape[2]
    assert d == LANES and H * d == HGH * LANES
    W8 = H * d
    rank = cache_mla_ckv.shape[-1]
    rope = cache_mla_kpe.shape[-1]
    Hm = od_w_uk.shape[2]
    nope = od_w_uk.shape[3]
    mem_len = mem_prompt.shape[1]
    Hx = cache_mem_k.shape[3]
    XW = Hx * cache_mem_k.shape[4]
    assert nope == LANES and od_w_uv.shape[3] == LANES and rope * 2 == LANES

    x = jnp.concatenate([x_prompt.reshape(NP, D), x_sample.reshape(NS, D)], axis=0)

    half = rope // 2
    pos = jnp.concatenate([jnp.tile(jnp.arange(S), B), jnp.tile(past_len + jnp.arange(T), Bd)]).astype(F32)
    inv_freq = ROPE_THETA ** (-jnp.arange(half, dtype=F32) / half)
    ang = pos[:, None] * inv_freq[None, :]
    cos, sin = jnp.cos(ang), jnp.sin(ang)
    zero = jnp.zeros_like(cos)
    rope_tabs = (jnp.concatenate([cos, cos, zero, zero], axis=1),
                 jnp.concatenate([-sin, zero, zero, zero], axis=1),
                 jnp.concatenate([zero, sin, zero, zero], axis=1))

    outs = {k: [] for k in ("p_fk", "p_fv", "p_flf", "p_hs", "p_ckv", "p_kpe", "p_mk", "p_mv",
                            "s_fk", "s_fv", "s_flf", "s_hs", "s_ckv", "s_kpe")}
    for layer in range(depth):
        if layer % 2 == 0:
            e = layer // 2
            w = ev_w_in[e]
            o_fq, o_fk, o_fv, o_fz, o_hq = 0, W8, W8 + G * d, W8 + 2 * G * d, W8 + 2 * G * d + H
            w_re = jnp.concatenate([w[:, o_hq:], w[:, o_fq:o_fz], _pad_cols(w[:, o_fz:o_hq], LANES)], axis=1)
            c_fq, c_fk, c_fv, c_fz = 4 * W8, 5 * W8, 5 * W8 + G * d, 5 * W8 + 2 * G * d
            n_cols = c_fz + LANES
            n_pad = -(-n_cols // 768) * 768
            z = norm_mm(x, ev_norm[e], _pad_cols(w_re, n_pad), name="even_in")
            bias = jnp.pad(ev_fox_bf[e], (0, LANES - H))
            zf = z[:, c_fz:c_fz + LANES]
            lf_p, cum_p = fox_gates(zf[:NP], bias, S, name="fox_gates_prompt")
            lf_s, cum_s = fox_gates(zf[NP:], bias, T, name="fox_gates_sample")
            lb = jnp.cumsum(jax.nn.softmax(ev_hg_gamma.astype(F32), axis=0), axis=0)[e]
            cq = cum_p[:, :H].reshape(B, S, H).transpose(0, 2, 1).reshape(B * H, 1, S)
            fo_p = flash_causal(z, lambda h: c_fq // d + h, d,
                                [(z, lambda g: c_fk // d + g, d)], z, lambda g: c_fv // d + g, d,
                                B=B, S=S, H=H, R=H // G, scale=d ** -0.5, cum=(cq, cum_p), name="fox_prompt")
            hg_p, hs_p = hgrn2(z, (0, 1, 2, 3), lb, ev_hg_norm[e], None, row0=0, n_seq=B, T=S, nb=1, H=HGH,
                               name="hgrn_prompt")
            lft = cache_fox_lf[e].transpose(0, 2, 1)
            cum_st = jnp.pad(cum_s[:, :H].reshape(Bd, T, H).transpose(0, 2, 1), ((0, 0), (0, 0), (0, LANES - T)))
            fo_s = fox_decode(z, c_fq // W8, c_fk // (G * d), c_fv // (G * d), cum_s, cum_st,
                              cache_fox_k[e].reshape(-1, page * G, d), cache_fox_v[e].reshape(-1, page * G, d),
                              lft, page_table, row0=NP, T=T, H=H, G=G)
            hg_s, hs_s = hgrn2(z, (0, 1, 2, 3), lb, ev_hg_norm[e], state_hgrn[e], row0=NP, n_seq=Bd, T=T, nb=2,
                               H=HGH, name="hgrn_sample")
            cat = jnp.concatenate([jnp.concatenate([fo_p, fo_s], axis=0), jnp.concatenate([hg_p, hg_s], axis=0)], axis=1)
            x = mm(cat, ev_w_out, x, layer=e, name="even_out")
            fk = z[:, c_fk:c_fk + G * d]
            fv = z[:, c_fv:c_fv + G * d]
            outs["p_fk"].append(fk[:NP].reshape(B, S, G, d))
            outs["p_fv"].append(fv[:NP].reshape(B, S, G, d))
            outs["p_flf"].append(lf_p[:, :H].reshape(B, S, H))
            outs["p_hs"].append(hs_p)
            outs["s_fk"].append(fk[NP:].reshape(Bd, T, G, d))
            outs["s_fv"].append(fv[NP:].reshape(Bd, T, G, d))
            outs["s_flf"].append(lf_s[:, :H].reshape(Bd, T, H))
            outs["s_hs"].append(hs_s)
        else:
            o = layer // 2
            scale = (nope + rope) ** -0.5
            n_in = 2 * rank + LANES
            z1 = norm_mm(x, od_norm[o], _pad_cols(od_w_in[o], n_in), name="mla_in")
            qa_n, ckv_n, kpe_r = mla_prep(z1, od_q_norm[o], od_kv_norm[o], rope_tabs, rank=rank)
            wq = od_w_qb[o].reshape(rank, Hm, nope + rope)
            wq = jnp.pad(wq, ((0, 0), (0, 0), (0, 2 * LANES - nope - rope))).reshape(rank, Hm * 2 * LANES)
            q_full = mla_qproj(qa_n, wq, rope_tabs, scale=scale)
            w_kv = jnp.concatenate([od_w_uk[o].reshape(rank, Hm * nope), od_w_uv[o].reshape(rank, Hm * LANES)], axis=1)
            kv = mm(ckv_n[:NP], w_kv, out_dtype=BF16, name="mla_kv_up")
            at_p = flash_causal(q_full, lambda h: h, 2 * LANES,
                                [(kv, lambda g: g, LANES), (kpe_r, lambda g: 0, LANES)], kv, lambda g: Hm + g, LANES,
                                B=B, S=S, H=Hm, R=1, scale=1.0, name="mla_prompt")
            q_cat = mla_qlat(q_full, od_w_uk[o].transpose(1, 2, 0), row0=NP, rows=NS, rank=rank)
            o_lat = mla_decode(q_cat, ckv_n, kpe_r, cache_mla_ckv[o], cache_mla_kpe[o].transpose(0, 2, 1), page_table,
                               row0=NP, T=T, rope=rope)
            at_s = mla_uv(o_lat, od_w_uv[o].reshape(rank, Hm * LANES))
            x = mm(jnp.concatenate([at_p, at_s], axis=0), od_w_out, x, layer=o, name="mla_out")
            outs["p_ckv"].append(ckv_n[:NP].reshape(B, S, rank))
            outs["p_kpe"].append(kpe_r[:NP, :rope].reshape(B, S, rope))
            outs["s_ckv"].append(ckv_n[NP:].reshape(Bd, T, rank))
            outs["s_kpe"].append(kpe_r[NP:, :rope].reshape(Bd, T, rope))
        w_mkv = jnp.concatenate([xa_wk[layer], xa_wv[layer]], axis=1)
        mkv = norm_mm(mem_prompt.reshape(B * mem_len, D), xa_mem_norm[layer], w_mkv, name="mem_kv")
        mk, mv = mkv[:, :XW], mkv[:, XW:]
        outs["p_mk"].append(mk.reshape(B, mem_len, Hx, XW // Hx))
        outs["p_mv"].append(mv.reshape(B, mem_len, Hx, XW // Hx))
        xq = norm_mm(x, xa_norm[layer], xa_wq, layer=layer, name="mem_q")
        xo_p = mem_attend(xq, mk.reshape(B, mem_len, XW), mv.reshape(B, mem_len, XW), row0=0, rows_per_seq=S,
                          nb=1, r=512, Hm=Hx, head_rows=False, name="mem_attn_prompt")
        xo_s = mem_attend(xq, cache_mem_k.reshape(depth, Bd, mem_len * Hx, XW // Hx),
                          cache_mem_v.reshape(depth, Bd, mem_len * Hx, XW // Hx), layer=layer,
                          row0=NP, rows_per_seq=T, nb=8, r=T, Hm=Hx, head_rows=True, name="mem_attn_sample")
        x = mm(jnp.concatenate([xo_p, xo_s], axis=0), xa_wo, x, layer=layer, name="mem_out")
        hmid = norm_mm(x, ff_norm[layer], ff_w1, layer=layer, act="relu2", out_dtype=BF16, name="ff_up")
        x = mm(hmid, ff_w2, x, layer=layer, name="ff_down")
    y_p = rmsnorm_rows(x, final_norm, row0=0, rows=NP, name="final_norm_prompt")
    y_s = rmsnorm_rows(x, final_norm, row0=NP, rows=NS, name="final_norm_sample")
    st = lambda k: jnp.stack(outs[k])
    return (y_p.reshape(B, S, D), y_s.reshape(Bd, T, D),
            st("p_fk"), st("p_fv"), st("p_flf"), st("p_hs"), st("p_ckv"), st("p_kpe"), st("p_mk"), st("p_mv"),
            st("s_fk"), st("s_fv"), st("s_flf"), st("s_hs"), st("s_ckv"), st("s_kpe"))
```

```python
import functools

import jax
import jax.numpy as jnp
from jax import lax
from jax.experimental import pallas as pl
from jax.experimental.pallas import tpu as pltpu

F32 = jnp.float32
BF16 = jnp.bfloat16
EPS = 1e-6
ROPE_THETA = 10000.0
LANES = 128
VMEM_LIMIT_BYTES = 56 * 1024 * 1024
HG_CHUNK = 64
FOX_PAGES_PER_STEP = 32
MLA_PAGES_PER_STEP = 64
LOG2E = 1.4426950408889634
NEG_INF = float("-inf")


def _params(*sem):
    return pltpu.CompilerParams(dimension_semantics=sem, vmem_limit_bytes=VMEM_LIMIT_BYTES)


def _pick(n, candidates):
    for c in candidates:
        if n % c == 0:
            return c
    raise ValueError(f"no tile for {n} in {candidates}")


def _dot(a, b):
    return jnp.dot(a, b, preferred_element_type=F32)


def _dot_nt(a, b):
    return lax.dot_general(a, b, (((1,), (1,)), ((), ())), preferred_element_type=F32)


def _dot_tn(a, b):
    return lax.dot_general(a, b, (((0,), (0,)), ((), ())), preferred_element_type=F32)


def _split3_dot(mask_bf16, x):
    x1 = x.astype(BF16)
    r1 = x - x1.astype(F32)
    x2 = r1.astype(BF16)
    x3 = (r1 - x2.astype(F32)).astype(BF16)
    return _dot(mask_bf16, x1) + _dot(mask_bf16, x2) + _dot(mask_bf16, x3)


def _norm_mm_kernel(x_ref, g_ref, w_ref, o_ref, xn_ref, *, act):
    @pl.when(pl.program_id(1) == 0)
    def _():
        x = x_ref[...]
        inv = lax.rsqrt(jnp.mean(x * x, axis=-1, keepdims=True) + EPS)
        xn_ref[...] = (x * inv * g_ref[...]).astype(BF16)

    acc = _dot(xn_ref[...], w_ref[...].astype(BF16))
    if act == "relu2":
        acc = jnp.maximum(acc, 0.0)
        acc = acc * acc
    o_ref[...] = acc.astype(o_ref.dtype)


def _w_spec(layer, bk, bn, imap):
    if layer is None:
        return pl.BlockSpec((bk, bn), imap)
    return pl.BlockSpec((None, bk, bn), lambda *idx: (layer,) + imap(*idx))


def norm_mm(x, g, w, *, layer=None, act=None, out_dtype=F32, name="norm_mm"):
    M, K = x.shape
    N = w.shape[-1]
    bm = _pick(M, (1024, 512, 256))
    bn = _pick(N, (1024, 768, 512, 384, 256, 128))
    return pl.pallas_call(
        functools.partial(_norm_mm_kernel, act=act),
        grid=(M // bm, N // bn),
        in_specs=[
            pl.BlockSpec((bm, K), lambda i, j: (i, 0)),
            pl.BlockSpec((1, K), lambda i, j: (0, 0)),
            _w_spec(layer, K, bn, lambda i, j: (0, j)),
        ],
        out_specs=pl.BlockSpec((bm, bn), lambda i, j: (i, j)),
        out_shape=jax.ShapeDtypeStruct((M, N), out_dtype),
        scratch_shapes=[pltpu.VMEM((bm, K), BF16)],
        compiler_params=_params("parallel", "arbitrary"),
        name=name,
    )(x, g.reshape(1, K).astype(F32), w)


def _mm_kernel(*refs, has_res, nk):
    a_ref, w_ref = refs[0], refs[1]
    r_ref = refs[2] if has_res else None
    o_ref = refs[3] if has_res else refs[2]
    part = _dot(a_ref[...].astype(BF16), w_ref[...].astype(BF16))
    if nk == 1:
        if has_res:
            part = part + r_ref[...]
        o_ref[...] = part.astype(o_ref.dtype)
        return
    acc_ref = refs[-1]
    k = pl.program_id(2)

    @pl.when(k == 0)
    def _():
        acc_ref[...] = part

    @pl.when(k > 0)
    def _():
        acc_ref[...] += part

    @pl.when(k == nk - 1)
    def _():
        out = acc_ref[...]
        if has_res:
            out = out + r_ref[...]
        o_ref[...] = out.astype(o_ref.dtype)


def mm(a, w, res=None, *, layer=None, out_dtype=F32, name="mm"):
    M, K = a.shape
    N = w.shape[-1]
    bm = _pick(M, (1024, 512, 256))
    bn = _pick(N, (1024, 512, 384, 256, 128))
    bk = K if K <= 2048 else 2048
    nk = K // bk
    in_specs = [
        pl.BlockSpec((bm, bk), lambda i, j, k: (i, k)),
        _w_spec(layer, bk, bn, lambda i, j, k: (k, j)),
    ]
    args = [a, w]
    if res is not None:
        in_specs.append(pl.BlockSpec((bm, bn), lambda i, j, k: (i, j)))
        args.append(res)
    return pl.pallas_call(
        functools.partial(_mm_kernel, has_res=res is not None, nk=nk),
        grid=(M // bm, N // bn, nk),
        in_specs=in_specs,
        out_specs=pl.BlockSpec((bm, bn), lambda i, j, k: (i, j)),
        out_shape=jax.ShapeDtypeStruct((M, N), out_dtype),
        scratch_shapes=[pltpu.VMEM((bm, bn), F32)] if nk > 1 else [],
        compiler_params=_params("parallel", "parallel", "arbitrary"),
        name=name,
    )(*args)


def _rmsnorm_kernel(x_ref, g_ref, o_ref):
    x = x_ref[...]
    inv = lax.rsqrt(jnp.mean(x * x, axis=-1, keepdims=True) + EPS)
    o_ref[...] = x * inv * g_ref[...]


def rmsnorm_rows(x, g, *, row0, rows, name="rmsnorm"):
    K = x.shape[1]
    bm = _pick(rows, (512, 256))
    assert row0 % bm == 0
    r0 = row0 // bm
    return pl.pallas_call(
        _rmsnorm_kernel,
        grid=(rows // bm,),
        in_specs=[pl.BlockSpec((bm, K), lambda i: (r0 + i, 0)), pl.BlockSpec((1, K), lambda i: (0, 0))],
        out_specs=pl.BlockSpec((bm, K), lambda i: (i, 0)),
        out_shape=jax.ShapeDtypeStruct((rows, K), F32),
        compiler_params=_params("parallel"),
        name=name,
    )(x, g.reshape(1, K).astype(F32))


def _log_sigmoid(x):
    return jnp.minimum(x, 0.0) - jnp.log(1.0 + jnp.exp(-jnp.abs(x)))


def _gate_kernel(z_ref, b_ref, lf_ref, cum_ref, carry_ref, *, br, seg):
    lf = _log_sigmoid(z_ref[...] + b_ref[...])
    lf_ref[...] = lf
    row = lax.broadcasted_iota(jnp.int32, (br, br), 0)
    col = lax.broadcasted_iota(jnp.int32, (br, br), 1)
    if seg >= br:
        mask = col <= row
    else:
        mask = (col <= row) & ((row // seg) == (col // seg))
    cum = _split3_dot(jnp.where(mask, 1.0, 0.0).astype(BF16), lf)
    if seg > br:
        blocks_per_seg = seg // br

        @pl.when(pl.program_id(0) % blocks_per_seg == 0)
        def _():
            carry_ref[...] = jnp.zeros_like(carry_ref)

        cum = cum + carry_ref[...]
        carry_ref[...] = cum[br - 1:br, :]
    cum_ref[...] = cum


def fox_gates(zf, bias, seg, name="fox_gates"):
    M = zf.shape[0]
    br = _pick(M, (512, 256, 128))
    assert seg % br == 0 or br % seg == 0
    return pl.pallas_call(
        functools.partial(_gate_kernel, br=br, seg=seg),
        grid=(M // br,),
        in_specs=[pl.BlockSpec((br, LANES), lambda i: (i, 0)), pl.BlockSpec((1, LANES), lambda i: (0, 0))],
        out_specs=[pl.BlockSpec((br, LANES), lambda i: (i, 0)), pl.BlockSpec((br, LANES), lambda i: (i, 0))],
        out_shape=[jax.ShapeDtypeStruct((M, LANES), F32), jax.ShapeDtypeStruct((M, LANES), F32)],
        scratch_shapes=[pltpu.VMEM((1, LANES), F32)],
        compiler_params=_params("arbitrary"),
        name=name,
    )(zf, bias.reshape(1, LANES))


def _flash_kernel(*refs, n_kparts, has_cum, bq, bk, scale, nk, n_sub):
    q_ref = refs[0]
    k_refs = refs[1:1 + n_kparts]
    v_ref = refs[1 + n_kparts]
    pos = 2 + n_kparts
    if has_cum:
        cq_ref, ck_ref = refs[pos], refs[pos + 1]
        pos += 2
    o_ref = refs[pos]
    m_ref, l_ref, acc_ref, qt_s = refs[pos + 1:pos + 5]
    h = pl.program_id(1)
    i = pl.program_id(2)
    j = pl.program_id(3)
    hw = bq // n_sub
    last_j = (i + 1) * (bq // bk) - 1
    first_diag_j = i * (bq // bk)

    @pl.when(j == 0)
    def _():
        m_ref[...] = jnp.full_like(m_ref, NEG_INF)
        l_ref[...] = jnp.zeros_like(l_ref)
        acc_ref[...] = jnp.zeros_like(acc_ref)
        qt_s[...] = (q_ref[...].astype(F32) * (scale * LOG2E)).T.astype(BF16)

    def step(masked):
        if n_kparts == 1:
            k = k_refs[0][...].astype(BF16)
        else:
            k = jnp.concatenate([kr[...].astype(BF16) for kr in k_refs], axis=-1)
        v = v_ref[...].astype(BF16)
        if has_cum:
            sel = lax.broadcasted_iota(jnp.int32, (LANES, LANES), 0) == h
            ck_rep = _split3_dot_right(ck_ref[...], jnp.where(sel, 1.0, 0.0).astype(BF16))
            ck_rep = jnp.concatenate([ck_rep * LOG2E] * (hw // LANES), axis=1)
            cq_row = cq_ref[0] * LOG2E
        m_old = m_ref[...]
        sts = []
        for sub in range(n_sub):
            cols = slice(sub * hw, (sub + 1) * hw)
            st = _dot(k, qt_s[:, cols])
            if has_cum:
                st = st + cq_row[:, cols] - ck_rep
            if masked:
                key = lax.broadcasted_iota(jnp.int32, (bk, hw), 0) + (j * bk - i * bq)
                qry = lax.broadcasted_iota(jnp.int32, (bk, hw), 1) + sub * hw
                st = jnp.where(key <= qry, st, NEG_INF)
            sts.append(st)
        m_new = jnp.maximum(m_old, jnp.concatenate([jnp.max(st, axis=0, keepdims=True) for st in sts], axis=1))
        alpha = jnp.exp2(m_old - m_new)
        pts = [jnp.exp2(st - m_new[:, sub * hw:(sub + 1) * hw]) for sub, st in enumerate(sts)]
        l_ref[...] = l_ref[...] * alpha + jnp.concatenate([jnp.sum(pt, axis=0, keepdims=True) for pt in pts], axis=1)
        pv = jnp.concatenate([_dot_tn(v, pt.astype(BF16)) for pt in pts], axis=1)
        acc_ref[...] = acc_ref[...] * alpha + pv
        m_ref[...] = m_new

    @pl.when(j < first_diag_j)
    def _():
        step(False)

    @pl.when((j >= first_diag_j) & (j <= last_j))
    def _():
        step(True)

    @pl.when(j == nk - 1)
    def _():
        o_ref[...] = (acc_ref[...] / l_ref[...]).T.astype(o_ref.dtype)


def flash_causal(q_arr, q_col, dq, k_parts, v_arr, v_col, dv, *, B, S, H, R, scale, cum=None, name="flash"):
    bq, bk, n_sub = 1024, 512, 4
    nq, nk = S // bq, S // bk
    ratio = bq // bk

    def kv_row(b, i, j):
        return b * nk + jnp.minimum(j, (i + 1) * ratio - 1)

    in_specs = [pl.BlockSpec((bq, dq), lambda b, h, i, j: (b * nq + i, q_col(h)))]
    args = [q_arr]
    for arr, col, width in k_parts:
        in_specs.append(pl.BlockSpec((bk, width), lambda b, h, i, j, col=col: (kv_row(b, i, j), col(h // R))))
        args.append(arr)
    in_specs.append(pl.BlockSpec((bk, dv), lambda b, h, i, j: (kv_row(b, i, j), v_col(h // R))))
    args.append(v_arr)
    scratch = [pltpu.VMEM((1, bq), F32), pltpu.VMEM((1, bq), F32), pltpu.VMEM((dv, bq), F32),
               pltpu.VMEM((dq, bq), BF16)]
    if cum is not None:
        cq, ck = cum
        in_specs.append(pl.BlockSpec((1, 1, bq), lambda b, h, i, j: (b * H + h, 0, i)))
        in_specs.append(pl.BlockSpec((bk, LANES), lambda b, h, i, j: (kv_row(b, i, j), 0)))
        args += [cq, ck]
    return pl.pallas_call(
        functools.partial(_flash_kernel, n_kparts=len(k_parts), has_cum=cum is not None, bq=bq, bk=bk, scale=scale,
                          nk=nk, n_sub=n_sub),
        grid=(B, H, nq, nk),
        in_specs=in_specs,
        out_specs=pl.BlockSpec((bq, dv), lambda b, h, i, j: (b * nq + i, h)),
        out_shape=jax.ShapeDtypeStruct((B * S, H * dv), BF16),
        scratch_shapes=scratch,
        compiler_params=_params("parallel", "parallel", "parallel", "arbitrary"),
        name=name,
    )(*args)


def _cumsum_rows(x):
    c = x.shape[0]
    row = lax.broadcasted_iota(jnp.int32, x.shape, 0)
    s = 1
    while s < c:
        x = x + jnp.where(row >= s, pltpu.roll(x, s, 0), 0.0)
        s *= 2
    return x


def _hgrn_kernel(*refs, c, nb, H, has_init, n_chunks):
    q_ref, f_ref, i_ref, g_ref, lb_ref, nw_ref = refs[:6]
    pos = 6
    s0_ref = None
    if has_init:
        s0_ref = refs[pos]
        pos += 1
    o_ref, sout_ref, st_ref = refs[pos], refs[pos + 1], refs[pos + 2]
    ci = pl.program_id(1)
    d = LANES

    @pl.when(ci == 0)
    def _():
        if has_init:
            for n in range(nb):
                for h in range(H):
                    st_ref[n, h] = s0_ref[n, h].T
        else:
            st_ref[...] = jnp.zeros_like(st_ref)

    row = lax.broadcasted_iota(jnp.int32, (c, c), 0)
    col = lax.broadcasted_iota(jnp.int32, (c, c), 1)
    causal = col <= row
    nw = nw_ref[...]
    for n in range(nb):
        rows = slice(n * c, (n + 1) * c)
        for h in range(H):
            cols = slice(h * d, (h + 1) * d)
            lb = lb_ref[:, cols]
            f = lb + (1.0 - lb) * jax.nn.sigmoid(f_ref[rows, cols])
            kk = 1.0 - f
            b = _cumsum_rows(jnp.log(f))
            b_last = b[c - 1:c, :]
            q_e = (q_ref[rows, cols] * jnp.exp(b)).astype(BF16)
            k_e = (kk * jnp.exp(-b)).astype(BF16)
            v = i_ref[rows, cols].astype(BF16)
            att = jnp.where(causal, _dot_nt(q_e, k_e), 0.0).astype(BF16)
            st = st_ref[n, h]
            o = _dot(att, v) + _dot_nt(q_e, st.astype(BF16))
            kd = (kk * jnp.exp(b_last - b)).astype(BF16)
            st_ref[n, h] = st * jnp.exp(b_last) + _dot_tn(v, kd)
            on = o * lax.rsqrt(jnp.mean(o * o, axis=-1, keepdims=True) + EPS) * nw
            gate = g_ref[rows, cols]
            o_ref[rows, cols] = (on * (gate * jax.nn.sigmoid(gate))).astype(o_ref.dtype)

    @pl.when(ci == n_chunks - 1)
    def _():
        for n in range(nb):
            for h in range(H):
                sout_ref[n, h] = st_ref[n, h].T


def hgrn2(z, cols, lb, nw, s0, *, row0, n_seq, T, nb, H, name="hgrn2"):
    c = min(HG_CHUNK, T)
    n_chunks = T // c
    W = H * LANES
    rb = nb * c
    assert (n_chunks == 1 or nb == 1) and row0 % rb == 0 and n_seq % nb == 0
    r0 = row0 // rb

    def zspec(cb):
        return pl.BlockSpec((rb, W), lambda s, ci, cb=cb: (r0 + s * n_chunks + ci, cb))

    in_specs = [zspec(cb) for cb in cols] + [
        pl.BlockSpec((1, W), lambda s, ci: (0, 0)),
        pl.BlockSpec((1, LANES), lambda s, ci: (0, 0)),
    ]
    args = [z, z, z, z, lb.reshape(1, W), nw.reshape(1, LANES)]
    if s0 is not None:
        in_specs.append(pl.BlockSpec((nb, H, LANES, LANES), lambda s, ci: (s, 0, 0, 0)))
        args.append(s0)
    return pl.pallas_call(
        functools.partial(_hgrn_kernel, c=c, nb=nb, H=H, has_init=s0 is not None, n_chunks=n_chunks),
        grid=(n_seq // nb, n_chunks),
        in_specs=in_specs,
        out_specs=[
            pl.BlockSpec((rb, W), lambda s, ci: (s * n_chunks + ci, 0)),
            pl.BlockSpec((nb, H, LANES, LANES), lambda s, ci: (s, 0, 0, 0)),
        ],
        out_shape=[jax.ShapeDtypeStruct((n_seq * T, W), BF16), jax.ShapeDtypeStruct((n_seq, H, LANES, LANES), F32)],
        scratch_shapes=[pltpu.VMEM((nb, H, LANES, LANES), F32)],
        compiler_params=_params("parallel", "arbitrary"),
        name=name,
    )(*args)


def _page_copies(pt_ref, pools, bufs, sems, b, chunk, slot, P):
    out = []
    for jj in range(P):
        pg = pt_ref[b, chunk * P + jj]
        for a, (pool, buf) in enumerate(zip(pools, bufs)):
            out.append(pltpu.make_async_copy(pool.at[pg], buf.at[slot, jj], sems.at[a, slot]))
    return out


def _gather_pages(pt_ref, pools, bufs, sems, *, P, n_steps, n_seq, chunk_of_step):
    b = pl.program_id(0)
    step = pl.program_id(1)
    g = b * n_steps + step
    slot = lax.rem(g, 2)

    @pl.when(g == 0)
    def _():
        for c in _page_copies(pt_ref, pools, bufs, sems, 0, chunk_of_step(0), 0, P):
            c.start()

    @pl.when(g + 1 < n_seq * n_steps)
    def _():
        nxt = g + 1
        nb = nxt // n_steps
        for c in _page_copies(pt_ref, pools, bufs, sems, nb, chunk_of_step(nxt - nb * n_steps), 1 - slot, P):
            c.start()

    for c in _page_copies(pt_ref, pools, bufs, sems, b, chunk_of_step(step), slot, P):
        c.wait()
    return slot


def _fox_decode_kernel(pt_ref, q_ref, kn_ref, vn_ref, cn_ref, cnt_ref, k_hbm, v_hbm, lf_hbm, o_ref,
                       kbuf, vbuf, lfbuf, sems, qbd_s, m_s, l_s, acc_s, suf_s, *, P, T, H, G, scale, n_steps, n_seq):
    d = LANES
    R = H // G
    HT = H * T
    step = pl.program_id(1)
    slot = _gather_pages(pt_ref, (k_hbm, v_hbm, lf_hbm), (kbuf, vbuf, lfbuf), sems, P=P, n_steps=n_steps,
                         n_seq=n_seq, chunk_of_step=lambda s: n_steps - 1 - s)

    def page_rows(buf, jj):
        ref = buf.at[slot, jj]
        return jnp.concatenate([ref[pl.ds(g, LANES, stride=G), :] for g in range(G)], axis=1).astype(BF16)

    def head_rows(x):
        return jnp.concatenate([jnp.broadcast_to(x[h:h + 1, :], (T, x.shape[1])) for h in range(H)], axis=0)

    @pl.when(step == 0)
    def _():
        q = q_ref[...] * scale
        zero = jnp.zeros((T, d), F32)
        blocks = []
        for h in range(H):
            g = h // R
            blocks.append(jnp.concatenate([q[:, h * d:(h + 1) * d] if gg == g else zero for gg in range(G)], axis=1))
        qbd = jnp.concatenate(blocks, axis=0).astype(BF16)
        qbd_s[...] = qbd
        pad = jnp.zeros((LANES - T, G * d), F32)
        kn = jnp.concatenate([kn_ref[...], pad], axis=0).astype(BF16)
        vn = jnp.concatenate([vn_ref[...], pad], axis=0).astype(BF16)
        cn = cn_ref[...]
        cn_rows = jnp.concatenate([cn[:, h:h + 1] for h in range(H)], axis=0)
        cn_cols = head_rows(cnt_ref[...])
        s = _dot_nt(qbd, kn) + cn_rows - cn_cols
        t_idx = lax.broadcasted_iota(jnp.int32, (HT, LANES), 0) % T
        s_idx = lax.broadcasted_iota(jnp.int32, (HT, LANES), 1)
        s = jnp.where(s_idx <= t_idx, s, NEG_INF)
        m = jnp.max(s, axis=-1, keepdims=True)
        p = jnp.exp(s - m)
        m_s[...] = m
        l_s[...] = jnp.sum(p, axis=-1, keepdims=True)
        acc_s[...] = _dot(p.astype(BF16), vn)
        suf_s[...] = cn_rows

    tri = (lax.broadcasted_iota(jnp.int32, (LANES, LANES), 0) <= lax.broadcasted_iota(jnp.int32, (LANES, LANES), 1))
    tri = jnp.where(tri, 1.0, 0.0).astype(BF16)
    qbd = qbd_s[...]
    suf = suf_s[...]
    cw_all = _split3_dot_right(lfbuf[slot].reshape(P * H, LANES), tri)
    s_parts = [None] * P
    for jj in range(P - 1, -1, -1):
        cw_rows = head_rows(cw_all[jj * H:(jj + 1) * H, :])
        tot = cw_rows[:, LANES - 1:LANES]
        s = _dot_nt(qbd, page_rows(kbuf, jj))
        s_parts[jj] = s + (suf + tot) - cw_rows
        suf = suf + tot
    suf_s[...] = suf
    m_old = m_s[...]
    m_new = m_old
    for jj in range(P):
        m_new = jnp.maximum(m_new, jnp.max(s_parts[jj], axis=-1, keepdims=True))
    alpha = jnp.exp(m_old - m_new)
    l_new = l_s[...] * alpha
    acc = acc_s[...] * alpha
    for jj in range(P):
        p = jnp.exp(s_parts[jj] - m_new)
        l_new = l_new + jnp.sum(p, axis=-1, keepdims=True)
        acc = acc + _dot(p.astype(BF16), page_rows(vbuf, jj))
    m_s[...] = m_new
    l_s[...] = l_new
    acc_s[...] = acc

    @pl.when(step == n_steps - 1)
    def _():
        out = acc / l_new
        o_ref[...] = jnp.concatenate(
            [out[h * T:(h + 1) * T, (h // R) * d:(h // R + 1) * d] for h in range(H)], axis=1).astype(o_ref.dtype)


def _split3_dot_right(x, mask_bf16):
    x1 = x.astype(BF16)
    r1 = x - x1.astype(F32)
    x2 = r1.astype(BF16)
    x3 = (r1 - x2.astype(F32)).astype(BF16)
    return _dot(x1, mask_bf16) + _dot(x2, mask_bf16) + _dot(x3, mask_bf16)


def fox_decode(z, q_cb, k_cb, v_cb, cnew, cnew_t, pool_k, pool_v, pool_lft, page_table, *, row0, T, H, G, name="fox_decode"):
    Bd, n_pages = page_table.shape
    P = FOX_PAGES_PER_STEP
    n_steps = n_pages // P
    d = LANES
    page = pool_lft.shape[2]
    assert page == LANES and pool_k.shape[1] == page * G and n_pages % P == 0 and row0 % T == 0
    r0 = row0 // T
    hbm = pl.BlockSpec(memory_space=pl.ANY)
    in_specs = [
        pl.BlockSpec((T, H * d), lambda b, s, pt: (r0 + b, q_cb)),
        pl.BlockSpec((T, G * d), lambda b, s, pt: (r0 + b, k_cb)),
        pl.BlockSpec((T, G * d), lambda b, s, pt: (r0 + b, v_cb)),
        pl.BlockSpec((T, LANES), lambda b, s, pt: (b, 0)),
        pl.BlockSpec((None, H, LANES), lambda b, s, pt: (b, 0, 0)),
        hbm, hbm, hbm,
    ]
    HT = H * T
    grid_spec = pltpu.PrefetchScalarGridSpec(
        num_scalar_prefetch=1,
        grid=(Bd, n_steps),
        in_specs=in_specs,
        out_specs=pl.BlockSpec((T, H * d), lambda b, s, pt: (b, 0)),
        scratch_shapes=[pltpu.VMEM((2, P, page * G, d), F32), pltpu.VMEM((2, P, page * G, d), F32),
                        pltpu.VMEM((2, P, H, page), F32), pltpu.SemaphoreType.DMA((3, 2)),
                        pltpu.VMEM((HT, G * d), BF16), pltpu.VMEM((HT, 1), F32), pltpu.VMEM((HT, 1), F32),
                        pltpu.VMEM((HT, G * d), F32), pltpu.VMEM((HT, 1), F32)],
    )
    return pl.pallas_call(
        functools.partial(_fox_decode_kernel, P=P, T=T, H=H, G=G, scale=d ** -0.5, n_steps=n_steps, n_seq=Bd),
        grid_spec=grid_spec,
        out_shape=jax.ShapeDtypeStruct((Bd * T, H * d), BF16),
        compiler_params=_params("arbitrary", "arbitrary"),
        name=name,
    )(page_table, z, z, z, cnew, cnew_t, pool_k, pool_v, pool_lft)


def _rope_block(v, c, s1, s2):
    return v * c + pltpu.roll(v, 96, 1) * s1 + pltpu.roll(v, 32, 1) * s2


def _mla_prep_kernel(z_ref, qg_ref, kg_ref, c_ref, s1_ref, s2_ref, qa_ref, ckv_ref, kpe_ref, *, rank):
    qa = z_ref[:, :rank]
    inv = lax.rsqrt(jnp.mean(qa * qa, axis=-1, keepdims=True) + EPS)
    qa_ref[...] = (qa * inv * qg_ref[...]).astype(qa_ref.dtype)
    ckv = z_ref[:, rank:2 * rank]
    inv = lax.rsqrt(jnp.mean(ckv * ckv, axis=-1, keepdims=True) + EPS)
    ckv_ref[...] = ckv * inv * kg_ref[...]
    kpe_ref[...] = _rope_block(z_ref[:, 2 * rank:2 * rank + LANES], c_ref[...], s1_ref[...], s2_ref[...])


def mla_prep(z1, q_norm, kv_norm, rope_tabs, *, rank, name="mla_prep"):
    M, W = z1.shape
    bm = _pick(M, (1024, 512, 256))
    row = lambda i: (i, 0)
    fix = lambda i: (0, 0)
    return pl.pallas_call(
        functools.partial(_mla_prep_kernel, rank=rank),
        grid=(M // bm,),
        in_specs=[pl.BlockSpec((bm, W), row), pl.BlockSpec((1, rank), fix), pl.BlockSpec((1, rank), fix)]
        + [pl.BlockSpec((bm, LANES), row)] * 3,
        out_specs=[pl.BlockSpec((bm, rank), row), pl.BlockSpec((bm, rank), row), pl.BlockSpec((bm, LANES), row)],
        out_shape=[jax.ShapeDtypeStruct((M, rank), BF16), jax.ShapeDtypeStruct((M, rank), F32),
                   jax.ShapeDtypeStruct((M, LANES), F32)],
        compiler_params=_params("parallel"),
        name=name,
    )(z1, q_norm.reshape(1, rank), kv_norm.reshape(1, rank), *rope_tabs)


def _qproj_kernel(a_ref, w_ref, c_ref, s1_ref, s2_ref, o_ref, *, heads_per_block, scale):
    acc = _dot(a_ref[...], w_ref[...].astype(BF16)) * scale
    c, s1, s2 = c_ref[...], s1_ref[...], s2_ref[...]
    for hh in range(heads_per_block):
        base = hh * 2 * LANES
        o_ref[:, base:base + LANES] = acc[:, base:base + LANES].astype(o_ref.dtype)
        o_ref[:, base + LANES:base + 2 * LANES] = _rope_block(
            acc[:, base + LANES:base + 2 * LANES], c, s1, s2).astype(o_ref.dtype)


def mla_qproj(qa_n, w_qb_pad, rope_tabs, *, scale, name="mla_qproj"):
    M, K = qa_n.shape
    N = w_qb_pad.shape[1]
    bm = _pick(M, (1024, 512, 256))
    bn = 1024
    row = lambda i, j: (i, 0)
    return pl.pallas_call(
        functools.partial(_qproj_kernel, heads_per_block=bn // (2 * LANES), scale=scale),
        grid=(M // bm, N // bn),
        in_specs=[pl.BlockSpec((bm, K), row), pl.BlockSpec((K, bn), lambda i, j: (0, j))]
        + [pl.BlockSpec((bm, LANES), row)] * 3,
        out_specs=pl.BlockSpec((bm, bn), lambda i, j: (i, j)),
        out_shape=jax.ShapeDtypeStruct((M, N), BF16),
        compiler_params=_params("parallel", "parallel"),
        name=name,
    )(qa_n, w_qb_pad, *rope_tabs)


def _qlat_kernel(q_ref, w_ref, o_ref, *, rank):
    q = q_ref[...]
    o_ref[0, :, :rank] = _dot(q[:, :LANES], w_ref[0].astype(BF16))
    o_ref[0, :, rank:] = q[:, LANES:].astype(F32)


def mla_qlat(q_full, w_ukt, *, row0, rows, rank, name="mla_qlat"):
    Hh = w_ukt.shape[0]
    r0 = row0 // rows
    assert row0 % rows == 0
    return pl.pallas_call(
        functools.partial(_qlat_kernel, rank=rank),
        grid=(Hh,),
        in_specs=[pl.BlockSpec((rows, 2 * LANES), lambda h: (r0, h)),
                  pl.BlockSpec((1, LANES, rank), lambda h: (h, 0, 0))],
        out_specs=pl.BlockSpec((1, rows, rank + LANES), lambda h: (h, 0, 0)),
        out_shape=jax.ShapeDtypeStruct((Hh, rows, rank + LANES), F32),
        compiler_params=_params("parallel"),
        name=name,
    )(q_full, w_ukt)


def _uv_kernel(x_ref, w_ref, o_ref):
    o_ref[...] = _dot(x_ref[0].astype(BF16), w_ref[...].astype(BF16)).astype(o_ref.dtype)


def mla_uv(o_lat, w_uv2d, name="mla_uv"):
    Hh, rows, rank = o_lat.shape
    return pl.pallas_call(
        _uv_kernel,
        grid=(Hh,),
        in_specs=[pl.BlockSpec((1, rows, rank), lambda h: (h, 0, 0)), pl.BlockSpec((rank, LANES), lambda h: (0, h))],
        out_specs=pl.BlockSpec((rows, LANES), lambda h: (0, h)),
        out_shape=jax.ShapeDtypeStruct((rows, Hh * LANES), BF16),
        compiler_params=_params("parallel"),
        name=name,
    )(o_lat, w_uv2d)


def _mla_decode_kernel(pt_ref, q_ref, cn_ref, kn_ref, c_hbm, r_hbm, o_ref, cbuf, rbuf, sems,
                       ql_s, qr_s, m_s, l_s, acc_s, *, P, T, Hh, rank, rope, n_steps, n_seq):
    HT = Hh * T
    step = pl.program_id(1)
    slot = _gather_pages(pt_ref, (c_hbm, r_hbm), (cbuf, rbuf), sems, P=P, n_steps=n_steps, n_seq=n_seq,
                         chunk_of_step=lambda s: s)

    @pl.when(step == 0)
    def _():
        q = q_ref[...].reshape(HT, rank + LANES)
        ql = q[:, :rank].astype(BF16)
        qr = q[:, rank:rank + rope].astype(BF16)
        ql_s[...] = ql
        qr_s[...] = qr
        cn = jnp.concatenate([cn_ref[...], jnp.zeros((LANES - T, rank), F32)], axis=0).astype(BF16)
        kn = jnp.concatenate([kn_ref[:, :rope], jnp.zeros((LANES - T, rope), F32)], axis=0).astype(BF16)
        s = _dot_nt(ql, cn) + _dot_nt(qr, kn)
        t_idx = lax.broadcasted_iota(jnp.int32, (HT, LANES), 0) % T
        s_idx = lax.broadcasted_iota(jnp.int32, (HT, LANES), 1)
        s = jnp.where(s_idx <= t_idx, s, NEG_INF)
        m = jnp.max(s, axis=-1, keepdims=True)
        p = jnp.exp(s - m)
        m_s[...] = m
        l_s[...] = jnp.sum(p, axis=-1, keepdims=True)
        acc_s[...] = _dot(p.astype(BF16), cn)

    ql = ql_s[...]
    qr = qr_s[...]
    n_pairs = P // 2
    ckv = [jnp.concatenate([cbuf[slot, 2 * jj].astype(BF16), cbuf[slot, 2 * jj + 1].astype(BF16)], axis=0)
           for jj in range(n_pairs)]
    s_parts = []
    for jj in range(n_pairs):
        kpe_t = jnp.concatenate([rbuf[slot, 2 * jj], rbuf[slot, 2 * jj + 1]], axis=1).astype(BF16)
        s_parts.append(_dot_nt(ql, ckv[jj]) + _dot(qr, kpe_t))
    m_old = m_s[...]
    m_new = m_old
    for jj in range(n_pairs):
        m_new = jnp.maximum(m_new, jnp.max(s_parts[jj], axis=-1, keepdims=True))
    alpha = jnp.exp(m_old - m_new)
    l_new = l_s[...] * alpha
    acc = acc_s[...] * alpha
    for jj in range(n_pairs):
        p = jnp.exp(s_parts[jj] - m_new)
        l_new = l_new + jnp.sum(p, axis=-1, keepdims=True)
        acc = acc + _dot(p.astype(BF16), ckv[jj])
    m_s[...] = m_new
    l_s[...] = l_new
    acc_s[...] = acc

    @pl.when(step == n_steps - 1)
    def _():
        o_ref[...] = (acc / l_new).reshape(Hh, T, rank)


def mla_decode(q_cat, ckv_n, kpe_r, pool_ckv, pool_kpe_t, page_table, *, row0, T, rope, name="mla_decode"):
    Hh = q_cat.shape[0]
    rank = pool_ckv.shape[-1]
    Bd, n_pages = page_table.shape
    P = MLA_PAGES_PER_STEP
    n_steps = n_pages // P
    page = pool_ckv.shape[1]
    assert page == LANES and pool_kpe_t.shape[1:] == (rope, page) and n_pages % P == 0 and P % 2 == 0 and row0 % T == 0
    r0 = row0 // T
    hbm = pl.BlockSpec(memory_space=pl.ANY)
    in_specs = [
        pl.BlockSpec((Hh, T, rank + LANES), lambda b, s, pt: (0, b, 0)),
        pl.BlockSpec((T, rank), lambda b, s, pt: (r0 + b, 0)),
        pl.BlockSpec((T, LANES), lambda b, s, pt: (r0 + b, 0)),
        hbm, hbm,
    ]
    HT = Hh * T
    grid_spec = pltpu.PrefetchScalarGridSpec(
        num_scalar_prefetch=1,
        grid=(Bd, n_steps),
        in_specs=in_specs,
        out_specs=pl.BlockSpec((Hh, T, rank), lambda b, s, pt: (0, b, 0)),
        scratch_shapes=[pltpu.VMEM((2, P, page, rank), F32), pltpu.VMEM((2, P, rope, page), F32),
                        pltpu.SemaphoreType.DMA((2, 2)),
                        pltpu.VMEM((HT, rank), BF16), pltpu.VMEM((HT, rope), BF16), pltpu.VMEM((HT, 1), F32),
                        pltpu.VMEM((HT, 1), F32), pltpu.VMEM((HT, rank), F32)],
    )
    return pl.pallas_call(
        functools.partial(_mla_decode_kernel, P=P, T=T, Hh=Hh, rank=rank, rope=rope, n_steps=n_steps, n_seq=Bd),
        grid_spec=grid_spec,
        out_shape=jax.ShapeDtypeStruct((Hh, Bd * T, rank), F32),
        compiler_params=_params("arbitrary", "arbitrary"),
        name=name,
    )(page_table, q_cat, ckv_n, kpe_r, pool_ckv, pool_kpe_t)


def _mem_attn_kernel(q_ref, k_ref, v_ref, o_ref, *, nb, r, Hm, scale, head_rows, mem_len):
    d = LANES
    pairs = [(n, h) for n in range(nb) for h in range(Hm)]

    def mem(ref, n, h):
        if head_rows:
            return ref[n, pl.ds(h, mem_len, stride=Hm), :].astype(BF16)
        return ref[n, :, h * d:(h + 1) * d].astype(BF16)

    q_all = (q_ref[...] * scale).astype(BF16)
    s_all = [_dot_nt(q_all[n * r:(n + 1) * r, h * d:(h + 1) * d], mem(k_ref, n, h)) for n, h in pairs]
    p_all, l_all = [], []
    for s in s_all:
        p = jnp.exp(s - jnp.max(s, axis=-1, keepdims=True))
        p_all.append(p.astype(BF16))
        l_all.append(jnp.sum(p, axis=-1, keepdims=True))
    o_all = [_dot(p, mem(v_ref, n, h)) / l for (n, h), p, l in zip(pairs, p_all, l_all)]
    o_ref[...] = jnp.concatenate(
        [jnp.concatenate(o_all[n * Hm:(n + 1) * Hm], axis=1) for n in range(nb)], axis=0).astype(o_ref.dtype)


def mem_attend(q, mk, mv, *, row0, rows_per_seq, nb, r, Hm, head_rows, layer=None, name="mem_attn"):
    n_seq, M, W = mk.shape[-3:]
    mem_len = M // Hm if head_rows else M
    blocks_per_seq = rows_per_seq // r
    assert (nb == 1 or blocks_per_seq == 1) and row0 % (nb * r) == 0 and n_seq % nb == 0
    r0 = row0 // (nb * r)
    n_blocks = n_seq * blocks_per_seq // nb
    if layer is None:
        kv_spec = pl.BlockSpec((nb, M, W), lambda i: (i // blocks_per_seq, 0, 0))
    else:
        kv_spec = pl.BlockSpec((None, nb, M, W), lambda i: (layer, i // blocks_per_seq, 0, 0))
    QW = Hm * LANES
    return pl.pallas_call(
        functools.partial(_mem_attn_kernel, nb=nb, r=r, Hm=Hm, scale=LANES ** -0.5, head_rows=head_rows,
                          mem_len=mem_len),
        grid=(n_blocks,),
        in_specs=[pl.BlockSpec((nb * r, QW), lambda i: (r0 + i, 0)), kv_spec, kv_spec],
        out_specs=pl.BlockSpec((nb * r, QW), lambda i: (i, 0)),
        out_shape=jax.ShapeDtypeStruct((n_seq * rows_per_seq, QW), BF16),
        compiler_params=_params("parallel"),
        name=name,
    )(q, mk, mv)


def _pad_cols(w, n):
    return jnp.pad(w, ((0, 0), (0, n - w.shape[1])))


def kernel(x_prompt, x_sample, mem_prompt, page_table, cache_fox_k, cache_fox_v, cache_fox_lf, state_hgrn, cache_mla_ckv, cache_mla_kpe, cache_mem_k, cache_mem_v, ev_norm, ev_w_in, ev_fox_bf, ev_hg_gamma, ev_hg_norm, ev_w_out, od_norm, od_w_in, od_q_norm, od_w_qb, od_kv_norm, od_w_uk, od_w_uv, od_w_out, xa_norm, xa_mem_norm, xa_wq, xa_wk, xa_wv, xa_wo, ff_norm, ff_w1, ff_w2, final_norm):
    B, S, D = x_prompt.shape
    Bd, T, _ = x_sample.shape
    depth = ff_w1.shape[0]
    n_pages = page_table.shape[1]
    page = cache_fox_k.shape[2]
    past_len = n_pages * page
    NP, NS = B * S, Bd * T
    N = NP + NS
    G = cache_fox_k.shape[3]
    d = cache_fox_k.shape[4]
    H = cache_fox_lf.shape[3]
    HGH = state_hgrn.shape[2]
    assert d == LANES and H * d == HGH * LANES
    W8 = H * d
    rank = cache_mla_ckv.shape[-1]
    rope = cache_mla_kpe.shape[-1]
    Hm = od_w_uk.shape[2]
    nope = od_w_uk.shape[3]
    mem_len = mem_prompt.shape[1]
    Hx = cache_mem_k.shape[3]
    XW = Hx * cache_mem_k.shape[4]
    assert nope == LANES and od_w_uv.shape[3] == LANES and rope * 2 == LANES

    x = jnp.concatenate([x_prompt.reshape(NP, D), x_sample.reshape(NS, D)], axis=0)

    half = rope // 2
    pos = jnp.concatenate([jnp.tile(jnp.arange(S), B), jnp.tile(past_len + jnp.arange(T), Bd)]).astype(F32)
    inv_freq = ROPE_THETA ** (-jnp.arange(half, dtype=F32) / half)
    ang = pos[:, None] * inv_freq[None, :]
    cos, sin = jnp.cos(ang), jnp.sin(ang)
    zero = jnp.zeros_like(cos)
    rope_tabs = (jnp.concatenate([cos, cos, zero, zero], axis=1),
                 jnp.concatenate([-sin, zero, zero, zero], axis=1),
                 jnp.concatenate([zero, sin, zero, zero], axis=1))

    outs = {k: [] for k in ("p_fk", "p_fv", "p_flf", "p_hs", "p_ckv", "p_kpe", "p_mk", "p_mv",
                            "s_fk", "s_fv", "s_flf", "s_hs", "s_ckv", "s_kpe")}
    for layer in range(depth):
        if layer % 2 == 0:
            e = layer // 2
            w = ev_w_in[e]
            o_fq, o_fk, o_fv, o_fz, o_hq = 0, W8, W8 + G * d, W8 + 2 * G * d, W8 + 2 * G * d + H
            w_re = jnp.concatenate([w[:, o_hq:], w[:, o_fq:o_fz], _pad_cols(w[:, o_fz:o_hq], LANES)], axis=1)
            c_fq, c_fk, c_fv, c_fz = 4 * W8, 5 * W8, 5 * W8 + G * d, 5 * W8 + 2 * G * d
            n_cols = c_fz + LANES
            n_pad = -(-n_cols // 768) * 768
            z = norm_mm(x, ev_norm[e], _pad_cols(w_re, n_pad), name="even_in")
            bias = jnp.pad(ev_fox_bf[e], (0, LANES - H))
            zf = z[:, c_fz:c_fz + LANES]
            lf_p, cum_p = fox_gates(zf[:NP], bias, S, name="fox_gates_prompt")
            lf_s, cum_s = fox_gates(zf[NP:], bias, T, name="fox_gates_sample")
            lb = jnp.cumsum(jax.nn.softmax(ev_hg_gamma.astype(F32), axis=0), axis=0)[e]
            cq = cum_p[:, :H].reshape(B, S, H).transpose(0, 2, 1).reshape(B * H, 1, S)
            fo_p = flash_causal(z, lambda h: c_fq // d + h, d,
                                [(z, lambda g: c_fk // d + g, d)], z, lambda g: c_fv // d + g, d,
                                B=B, S=S, H=H, R=H // G, scale=d ** -0.5, cum=(cq, cum_p), name="fox_prompt")
            hg_p, hs_p = hgrn2(z, (0, 1, 2, 3), lb, ev_hg_norm[e], None, row0=0, n_seq=B, T=S, nb=1, H=HGH,
                               name="hgrn_prompt")
            lft = cache_fox_lf[e].transpose(0, 2, 1)
            cum_st = jnp.pad(cum_s[:, :H].reshape(Bd, T, H).transpose(0, 2, 1), ((0, 0), (0, 0), (0, LANES - T)))
            fo_s = fox_decode(z, c_fq // W8, c_fk // (G * d), c_fv // (G * d), cum_s, cum_st,
                              cache_fox_k[e].reshape(-1, page * G, d), cache_fox_v[e].reshape(-1, page * G, d),
                              lft, page_table, row0=NP, T=T, H=H, G=G)
            hg_s, hs_s = hgrn2(z, (0, 1, 2, 3), lb, ev_hg_norm[e], state_hgrn[e], row0=NP, n_seq=Bd, T=T, nb=2,
                               H=HGH, name="hgrn_sample")
            cat = jnp.concatenate([jnp.concatenate([fo_p, fo_s], axis=0), jnp.concatenate([hg_p, hg_s], axis=0)], axis=1)
            x = mm(cat, ev_w_out, x, layer=e, name="even_out")
            fk = z[:, c_fk:c_fk + G * d]
            fv = z[:, c_fv:c_fv + G * d]
            outs["p_fk"].append(fk[:NP].reshape(B, S, G, d))
            outs["p_fv"].append(fv[:NP].reshape(B, S, G, d))
            outs["p_flf"].append(lf_p[:, :H].reshape(B, S, H))
            outs["p_hs"].append(hs_p)
            outs["s_fk"].append(fk[NP:].reshape(Bd, T, G, d))
            outs["s_fv"].append(fv[NP:].reshape(Bd, T, G, d))
            outs["s_flf"].append(lf_s[:, :H].reshape(Bd, T, H))
            outs["s_hs"].append(hs_s)
        else:
            o = layer // 2
            scale = (nope + rope) ** -0.5
            n_in = 2 * rank + LANES
            z1 = norm_mm(x, od_norm[o], _pad_cols(od_w_in[o], n_in), name="mla_in")
            qa_n, ckv_n, kpe_r = mla_prep(z1, od_q_norm[o], od_kv_norm[o], rope_tabs, rank=rank)
            wq = od_w_qb[o].reshape(rank, Hm, nope + rope)
            wq = jnp.pad(wq, ((0, 0), (0, 0), (0, 2 * LANES - nope - rope))).reshape(rank, Hm * 2 * LANES)
            q_full = mla_qproj(qa_n, wq, rope_tabs, scale=scale)
            w_kv = jnp.concatenate([od_w_uk[o].reshape(rank, Hm * nope), od_w_uv[o].reshape(rank, Hm * LANES)], axis=1)
            kv = mm(ckv_n[:NP], w_kv, out_dtype=BF16, name="mla_kv_up")
            at_p = flash_causal(q_full, lambda h: h, 2 * LANES,
                                [(kv, lambda g: g, LANES), (kpe_r, lambda g: 0, LANES)], kv, lambda g: Hm + g, LANES,
                                B=B, S=S, H=Hm, R=1, scale=1.0, name="mla_prompt")
            q_cat = mla_qlat(q_full, od_w_uk[o].transpose(1, 2, 0), row0=NP, rows=NS, rank=rank)
            o_lat = mla_decode(q_cat, ckv_n, kpe_r, cache_mla_ckv[o], cache_mla_kpe[o].transpose(0, 2, 1), page_table,
                               row0=NP, T=T, rope=rope)
            at_s = mla_uv(o_lat, od_w_uv[o].reshape(rank, Hm * LANES))
            x = mm(jnp.concatenate([at_p, at_s], axis=0), od_w_out, x, layer=o, name="mla_out")
            outs["p_ckv"].append(ckv_n[:NP].reshape(B, S, rank))
            outs["p_kpe"].append(kpe_r[:NP, :rope].reshape(B, S, rope))
            outs["s_ckv"].append(ckv_n[NP:].reshape(Bd, T, rank))
            outs["s_kpe"].append(kpe_r[NP:, :rope].reshape(Bd, T, rope))
        w_mkv = jnp.concatenate([xa_wk[layer], xa_wv[layer]], axis=1)
        mkv = norm_mm(mem_prompt.reshape(B * mem_len, D), xa_mem_norm[layer], w_mkv, name="mem_kv")
        mk, mv = mkv[:, :XW], mkv[:, XW:]
        outs["p_mk"].append(mk.reshape(B, mem_len, Hx, XW // Hx))
        outs["p_mv"].append(mv.reshape(B, mem_len, Hx, XW // Hx))
        xq = norm_mm(x, xa_norm[layer], xa_wq, layer=layer, name="mem_q")
        xo_p = mem_attend(xq, mk.reshape(B, mem_len, XW), mv.reshape(B, mem_len, XW), row0=0, rows_per_seq=S,
                          nb=1, r=512, Hm=Hx, head_rows=False, name="mem_attn_prompt")
        xo_s = mem_attend(xq, cache_mem_k.reshape(depth, Bd, mem_len * Hx, XW // Hx),
                          cache_mem_v.reshape(depth, Bd, mem_len * Hx, XW // Hx), layer=layer,
                          row0=NP, rows_per_seq=T, nb=8, r=T, Hm=Hx, head_rows=True, name="mem_attn_sample")
        x = mm(jnp.concatenate([xo_p, xo_s], axis=0), xa_wo, x, layer=layer, name="mem_out")
        hmid = norm_mm(x, ff_norm[layer], ff_w1, layer=layer, act="relu2", out_dtype=BF16, name="ff_up")
        x = mm(hmid, ff_w2, x, layer=layer, name="ff_down")
    y_p = rmsnorm_rows(x, final_norm, row0=0, rows=NP, name="final_norm_prompt")
    y_s = rmsnorm_rows(x, final_norm, row0=NP, rows=NS, name="final_norm_sample")
    st = lambda k: jnp.stack(outs[k])
    return (y_p.reshape(B, S, D), y_s.reshape(Bd, T, D),
            st("p_fk"), st("p_fv"), st("p_flf"), st("p_hs"), st("p_ckv"), st("p_kpe"), st("p_mk"), st("p_mv"),
            st("s_fk"), st("s_fv"), st("s_flf"), st("s_hs"), st("s_ckv"), st("s_kpe"))
```

```python
import functools

import jax
import jax.numpy as jnp
from jax import lax
from jax.experimental import pallas as pl
from jax.experimental.pallas import tpu as pltpu

F32 = jnp.float32
BF16 = jnp.bfloat16
EPS = 1e-6
ROPE_THETA = 10000.0
LANES = 128
VMEM_LIMIT_BYTES = 56 * 1024 * 1024
HG_CHUNK = 64
FOX_PAGES_PER_STEP = 32
MLA_PAGES_PER_STEP = 64
LOG2E = 1.4426950408889634
NEG_INF = float("-inf")


def _params(*sem):
    return pltpu.CompilerParams(dimension_semantics=sem, vmem_limit_bytes=VMEM_LIMIT_BYTES)


def _pick(n, candidates):
    for c in candidates:
        if n % c == 0:
            return c
    raise ValueError(f"no tile for {n} in {candidates}")


def _dot(a, b):
    return jnp.dot(a, b, preferred_element_type=F32)


def _dot_nt(a, b):
    return lax.dot_general(a, b, (((1,), (1,)), ((), ())), preferred_element_type=F32)


def _dot_tn(a, b):
    return lax.dot_general(a, b, (((0,), (0,)), ((), ())), preferred_element_type=F32)


def _split3_dot(mask_bf16, x):
    x1 = x.astype(BF16)
    r1 = x - x1.astype(F32)
    x2 = r1.astype(BF16)
    x3 = (r1 - x2.astype(F32)).astype(BF16)
    return _dot(mask_bf16, x1) + _dot(mask_bf16, x2) + _dot(mask_bf16, x3)


def _norm_mm_kernel(x_ref, g_ref, w_ref, o_ref, xn_ref, *, act):
    @pl.when(pl.program_id(1) == 0)
    def _():
        x = x_ref[...]
        inv = lax.rsqrt(jnp.mean(x * x, axis=-1, keepdims=True) + EPS)
        xn_ref[...] = (x * inv * g_ref[...]).astype(BF16)

    acc = _dot(xn_ref[...], w_ref[...].astype(BF16))
    if act == "relu2":
        acc = jnp.maximum(acc, 0.0)
        acc = acc * acc
    o_ref[...] = acc.astype(o_ref.dtype)


def _w_spec(layer, bk, bn, imap):
    if layer is None:
        return pl.BlockSpec((bk, bn), imap)
    return pl.BlockSpec((None, bk, bn), lambda *idx: (layer,) + imap(*idx))


def norm_mm(x, g, w, *, layer=None, act=None, out_dtype=F32, name="norm_mm"):
    M, K = x.shape
    N = w.shape[-1]
    bm = _pick(M, (1024, 512, 256))
    bn = _pick(N, (1024, 768, 512, 384, 256, 128))
    return pl.pallas_call(
        functools.partial(_norm_mm_kernel, act=act),
        grid=(M // bm, N // bn),
        in_specs=[
            pl.BlockSpec((bm, K), lambda i, j: (i, 0)),
            pl.BlockSpec((1, K), lambda i, j: (0, 0)),
            _w_spec(layer, K, bn, lambda i, j: (0, j)),
        ],
        out_specs=pl.BlockSpec((bm, bn), lambda i, j: (i, j)),
        out_shape=jax.ShapeDtypeStruct((M, N), out_dtype),
        scratch_shapes=[pltpu.VMEM((bm, K), BF16)],
        compiler_params=_params("parallel", "arbitrary"),
        name=name,
    )(x, g.reshape(1, K).astype(F32), w)


def _mm_kernel(*refs, has_res, nk):
    a_ref, w_ref = refs[0], refs[1]
    r_ref = refs[2] if has_res else None
    o_ref = refs[3] if has_res else refs[2]
    part = _dot(a_ref[...].astype(BF16), w_ref[...].astype(BF16))
    if nk == 1:
        if has_res:
            part = part + r_ref[...]
        o_ref[...] = part.astype(o_ref.dtype)
        return
    acc_ref = refs[-1]
    k = pl.program_id(2)

    @pl.when(k == 0)
    def _():
        acc_ref[...] = part

    @pl.when(k > 0)
    def _():
        acc_ref[...] += part

    @pl.when(k == nk - 1)
    def _():
        out = acc_ref[...]
        if has_res:
            out = out + r_ref[...]
        o_ref[...] = out.astype(o_ref.dtype)


def mm(a, w, res=None, *, layer=None, out_dtype=F32, name="mm"):
    M, K = a.shape
    N = w.shape[-1]
    bm = _pick(M, (1024, 512, 256))
    bn = _pick(N, (1024, 512, 384, 256, 128))
    bk = K if K <= 2048 else 2048
    nk = K // bk
    in_specs = [
        pl.BlockSpec((bm, bk), lambda i, j, k: (i, k)),
        _w_spec(layer, bk, bn, lambda i, j, k: (k, j)),
    ]
    args = [a, w]
    if res is not None:
        in_specs.append(pl.BlockSpec((bm, bn), lambda i, j, k: (i, j)))
        args.append(res)
    return pl.pallas_call(
        functools.partial(_mm_kernel, has_res=res is not None, nk=nk),
        grid=(M // bm, N // bn, nk),
        in_specs=in_specs,
        out_specs=pl.BlockSpec((bm, bn), lambda i, j, k: (i, j)),
        out_shape=jax.ShapeDtypeStruct((M, N), out_dtype),
        scratch_shapes=[pltpu.VMEM((bm, bn), F32)] if nk > 1 else [],
        compiler_params=_params("parallel", "parallel", "arbitrary"),
        name=name,
    )(*args)


def _rmsnorm_kernel(x_ref, g_ref, o_ref):
    x = x_ref[...]
    inv = lax.rsqrt(jnp.mean(x * x, axis=-1, keepdims=True) + EPS)
    o_ref[...] = x * inv * g_ref[...]


def rmsnorm_rows(x, g, *, row0, rows, name="rmsnorm"):
    K = x.shape[1]
    bm = _pick(rows, (512, 256))
    assert row0 % bm == 0
    r0 = row0 // bm
    return pl.pallas_call(
        _rmsnorm_kernel,
        grid=(rows // bm,),
        in_specs=[pl.BlockSpec((bm, K), lambda i: (r0 + i, 0)), pl.BlockSpec((1, K), lambda i: (0, 0))],
        out_specs=pl.BlockSpec((bm, K), lambda i: (i, 0)),
        out_shape=jax.ShapeDtypeStruct((rows, K), F32),
        compiler_params=_params("parallel"),
        name=name,
    )(x, g.reshape(1, K).astype(F32))


def _log_sigmoid(x):
    return jnp.minimum(x, 0.0) - jnp.log(1.0 + jnp.exp(-jnp.abs(x)))


def _gate_kernel(z_ref, b_ref, lf_ref, cum_ref, carry_ref, *, br, seg):
    lf = _log_sigmoid(z_ref[...] + b_ref[...])
    lf_ref[...] = lf
    row = lax.broadcasted_iota(jnp.int32, (br, br), 0)
    col = lax.broadcasted_iota(jnp.int32, (br, br), 1)
    if seg >= br:
        mask = col <= row
    else:
        mask = (col <= row) & ((row // seg) == (col // seg))
    cum = _split3_dot(jnp.where(mask, 1.0, 0.0).astype(BF16), lf)
    if seg > br:
        blocks_per_seg = seg // br

        @pl.when(pl.program_id(0) % blocks_per_seg == 0)
        def _():
            carry_ref[...] = jnp.zeros_like(carry_ref)

        cum = cum + carry_ref[...]
        carry_ref[...] = cum[br - 1:br, :]
    cum_ref[...] = cum


def fox_gates(zf, bias, seg, name="fox_gates"):
    M = zf.shape[0]
    br = _pick(M, (512, 256, 128))
    assert seg % br == 0 or br % seg == 0
    return pl.pallas_call(
        functools.partial(_gate_kernel, br=br, seg=seg),
        grid=(M // br,),
        in_specs=[pl.BlockSpec((br, LANES), lambda i: (i, 0)), pl.BlockSpec((1, LANES), lambda i: (0, 0))],
        out_specs=[pl.BlockSpec((br, LANES), lambda i: (i, 0)), pl.BlockSpec((br, LANES), lambda i: (i, 0))],
        out_shape=[jax.ShapeDtypeStruct((M, LANES), F32), jax.ShapeDtypeStruct((M, LANES), F32)],
        scratch_shapes=[pltpu.VMEM((1, LANES), F32)],
        compiler_params=_params("arbitrary"),
        name=name,
    )(zf, bias.reshape(1, LANES))


def _flash_kernel(*refs, n_kparts, has_cum, bq, bk, scale, nk, n_sub):
    q_ref = refs[0]
    k_refs = refs[1:1 + n_kparts]
    v_ref = refs[1 + n_kparts]
    pos = 2 + n_kparts
    if has_cum:
        cq_ref, ck_ref = refs[pos], refs[pos + 1]
        pos += 2
    o_ref = refs[pos]
    m_ref, l_ref, acc_ref, qt_s = refs[pos + 1:pos + 5]
    h = pl.program_id(1)
    i = pl.program_id(2)
    j = pl.program_id(3)
    hw = bq // n_sub
    last_j = (i + 1) * (bq // bk) - 1
    first_diag_j = i * (bq // bk)

    @pl.when(j == 0)
    def _():
        m_ref[...] = jnp.full_like(m_ref, NEG_INF)
        l_ref[...] = jnp.zeros_like(l_ref)
        acc_ref[...] = jnp.zeros_like(acc_ref)
        qt_s[...] = (q_ref[...].astype(F32) * (scale * LOG2E)).T.astype(BF16)

    def step(mode):
        if n_kparts == 1:
            k = k_refs[0][...].astype(BF16)
        else:
            k = jnp.concatenate([kr[...].astype(BF16) for kr in k_refs], axis=-1)
        v = v_ref[...].astype(BF16)
        if has_cum:
            sel = lax.broadcasted_iota(jnp.int32, (LANES, LANES), 0) == h
            ck_rep = _split3_dot_right(ck_ref[...], jnp.where(sel, 1.0, 0.0).astype(BF16))
            ck_rep = jnp.concatenate([ck_rep * LOG2E] * (hw // LANES), axis=1)
            cq_row = cq_ref[0] * LOG2E
        half = n_sub // 2
        active = range(half, n_sub) if mode == "diag_hi" else range(n_sub)
        lo = active[0] * hw
        m_old = m_ref[:, lo:]
        sts = []
        for sub in active:
            cols = slice(sub * hw, (sub + 1) * hw)
            st = _dot(k, qt_s[:, cols])
            if has_cum:
                st = st + cq_row[:, cols] - ck_rep
            if mode == "diag_hi" or (mode == "diag_lo" and sub < half):
                key = lax.broadcasted_iota(jnp.int32, (bk, hw), 0) + (j * bk - i * bq)
                qry = lax.broadcasted_iota(jnp.int32, (bk, hw), 1) + sub * hw
                st = jnp.where(key <= qry, st, NEG_INF)
            sts.append(st)
        m_new = jnp.maximum(m_old, jnp.concatenate([jnp.max(st, axis=0, keepdims=True) for st in sts], axis=1))
        alpha = jnp.exp2(m_old - m_new)
        pts = [jnp.exp2(st - m_new[:, n * hw:(n + 1) * hw]) for n, st in enumerate(sts)]
        l_ref[:, lo:] = l_ref[:, lo:] * alpha + jnp.concatenate(
            [jnp.sum(pt, axis=0, keepdims=True) for pt in pts], axis=1)
        pv = jnp.concatenate([_dot_tn(v, pt.astype(BF16)) for pt in pts], axis=1)
        acc_ref[:, lo:] = acc_ref[:, lo:] * alpha + pv
        m_ref[:, lo:] = m_new

    @pl.when(j < first_diag_j)
    def _():
        step("full")

    @pl.when(j == first_diag_j)
    def _():
        step("diag_lo")

    @pl.when(j == last_j)
    def _():
        step("diag_hi")

    @pl.when(j == nk - 1)
    def _():
        o_ref[...] = (acc_ref[...] / l_ref[...]).T.astype(o_ref.dtype)


def flash_causal(q_arr, q_col, dq, k_parts, v_arr, v_col, dv, *, B, S, H, R, scale, cum=None, name="flash"):
    bq, bk, n_sub = 1024, 512, 4
    nq, nk = S // bq, S // bk
    ratio = bq // bk
    assert ratio == 2 and n_sub % 2 == 0

    def kv_row(b, i, j):
        return b * nk + jnp.minimum(j, (i + 1) * ratio - 1)

    in_specs = [pl.BlockSpec((bq, dq), lambda b, h, i, j: (b * nq + i, q_col(h)))]
    args = [q_arr]
    for arr, col, width in k_parts:
        in_specs.append(pl.BlockSpec((bk, width), lambda b, h, i, j, col=col: (kv_row(b, i, j), col(h // R))))
        args.append(arr)
    in_specs.append(pl.BlockSpec((bk, dv), lambda b, h, i, j: (kv_row(b, i, j), v_col(h // R))))
    args.append(v_arr)
    scratch = [pltpu.VMEM((1, bq), F32), pltpu.VMEM((1, bq), F32), pltpu.VMEM((dv, bq), F32),
               pltpu.VMEM((dq, bq), BF16)]
    if cum is not None:
        cq, ck = cum
        in_specs.append(pl.BlockSpec((1, 1, bq), lambda b, h, i, j: (b * H + h, 0, i)))
        in_specs.append(pl.BlockSpec((bk, LANES), lambda b, h, i, j: (kv_row(b, i, j), 0)))
        args += [cq, ck]
    return pl.pallas_call(
        functools.partial(_flash_kernel, n_kparts=len(k_parts), has_cum=cum is not None, bq=bq, bk=bk, scale=scale,
                          nk=nk, n_sub=n_sub),
        grid=(B, H, nq, nk),
        in_specs=in_specs,
        out_specs=pl.BlockSpec((bq, dv), lambda b, h, i, j: (b * nq + i, h)),
        out_shape=jax.ShapeDtypeStruct((B * S, H * dv), BF16),
        scratch_shapes=scratch,
        compiler_params=_params("parallel", "parallel", "parallel", "arbitrary"),
        name=name,
    )(*args)


def _cumsum_rows(x):
    c = x.shape[0]
    row = lax.broadcasted_iota(jnp.int32, x.shape, 0)
    s = 1
    while s < c:
        x = x + jnp.where(row >= s, pltpu.roll(x, s, 0), 0.0)
        s *= 2
    return x


def _hgrn_kernel(*refs, c, nb, H, has_init, n_chunks):
    q_ref, f_ref, i_ref, g_ref, lb_ref, nw_ref = refs[:6]
    pos = 6
    s0_ref = None
    if has_init:
        s0_ref = refs[pos]
        pos += 1
    o_ref, sout_ref, st_ref = refs[pos], refs[pos + 1], refs[pos + 2]
    ci = pl.program_id(1)
    d = LANES

    @pl.when(ci == 0)
    def _():
        if has_init:
            for n in range(nb):
                for h in range(H):
                    st_ref[n, h] = s0_ref[n, h].T
        else:
            st_ref[...] = jnp.zeros_like(st_ref)

    row = lax.broadcasted_iota(jnp.int32, (c, c), 0)
    col = lax.broadcasted_iota(jnp.int32, (c, c), 1)
    causal = col <= row
    nw = nw_ref[...]
    for n in range(nb):
        rows = slice(n * c, (n + 1) * c)
        for h in range(H):
            cols = slice(h * d, (h + 1) * d)
            lb = lb_ref[:, cols]
            f = lb + (1.0 - lb) * jax.nn.sigmoid(f_ref[rows, cols])
            kk = 1.0 - f
            b = _cumsum_rows(jnp.log(f))
            b_last = b[c - 1:c, :]
            q_e = (q_ref[rows, cols] * jnp.exp(b)).astype(BF16)
            k_e = (kk * jnp.exp(-b)).astype(BF16)
            v = i_ref[rows, cols].astype(BF16)
            att = jnp.where(causal, _dot_nt(q_e, k_e), 0.0).astype(BF16)
            st = st_ref[n, h]
            o = _dot(att, v) + _dot_nt(q_e, st.astype(BF16))
            kd = (kk * jnp.exp(b_last - b)).astype(BF16)
            st_ref[n, h] = st * jnp.exp(b_last) + _dot_tn(v, kd)
            on = o * lax.rsqrt(jnp.mean(o * o, axis=-1, keepdims=True) + EPS) * nw
            gate = g_ref[rows, cols]
            o_ref[rows, cols] = (on * (gate * jax.nn.sigmoid(gate))).astype(o_ref.dtype)

    @pl.when(ci == n_chunks - 1)
    def _():
        for n in range(nb):
            for h in range(H):
                sout_ref[n, h] = st_ref[n, h].T


def hgrn2(z, cols, lb, nw, s0, *, row0, n_seq, T, nb, H, name="hgrn2"):
    c = min(HG_CHUNK, T)
    n_chunks = T // c
    W = H * LANES
    rb = nb * c
    assert (n_chunks == 1 or nb == 1) and row0 % rb == 0 and n_seq % nb == 0
    r0 = row0 // rb

    def zspec(cb):
        return pl.BlockSpec((rb, W), lambda s, ci, cb=cb: (r0 + s * n_chunks + ci, cb))

    in_specs = [zspec(cb) for cb in cols] + [
        pl.BlockSpec((1, W), lambda s, ci: (0, 0)),
        pl.BlockSpec((1, LANES), lambda s, ci: (0, 0)),
    ]
    args = [z, z, z, z, lb.reshape(1, W), nw.reshape(1, LANES)]
    if s0 is not None:
        in_specs.append(pl.BlockSpec((nb, H, LANES, LANES), lambda s, ci: (s, 0, 0, 0)))
        args.append(s0)
    return pl.pallas_call(
        functools.partial(_hgrn_kernel, c=c, nb=nb, H=H, has_init=s0 is not None, n_chunks=n_chunks),
        grid=(n_seq // nb, n_chunks),
        in_specs=in_specs,
        out_specs=[
            pl.BlockSpec((rb, W), lambda s, ci: (s * n_chunks + ci, 0)),
            pl.BlockSpec((nb, H, LANES, LANES), lambda s, ci: (s, 0, 0, 0)),
        ],
        out_shape=[jax.ShapeDtypeStruct((n_seq * T, W), BF16), jax.ShapeDtypeStruct((n_seq, H, LANES, LANES), F32)],
        scratch_shapes=[pltpu.VMEM((nb, H, LANES, LANES), F32)],
        compiler_params=_params("parallel", "arbitrary"),
        name=name,
    )(*args)


def _page_copies(pt_ref, pools, bufs, sems, b, chunk, slot, P):
    out = []
    for jj in range(P):
        pg = pt_ref[b, chunk * P + jj]
        for a, (pool, buf) in enumerate(zip(pools, bufs)):
            out.append(pltpu.make_async_copy(pool.at[pg], buf.at[slot, jj], sems.at[a, slot]))
    return out


def _gather_pages(pt_ref, pools, bufs, sems, *, P, n_steps, n_seq, chunk_of_step):
    b = pl.program_id(0)
    step = pl.program_id(1)
    g = b * n_steps + step
    slot = lax.rem(g, 2)

    @pl.when(g == 0)
    def _():
        for c in _page_copies(pt_ref, pools, bufs, sems, 0, chunk_of_step(0), 0, P):
            c.start()

    @pl.when(g + 1 < n_seq * n_steps)
    def _():
        nxt = g + 1
        nb = nxt // n_steps
        for c in _page_copies(pt_ref, pools, bufs, sems, nb, chunk_of_step(nxt - nb * n_steps), 1 - slot, P):
            c.start()

    for c in _page_copies(pt_ref, pools, bufs, sems, b, chunk_of_step(step), slot, P):
        c.wait()
    return slot


def _fox_decode_kernel(pt_ref, q_ref, kn_ref, vn_ref, cn_ref, cnt_ref, k_hbm, v_hbm, lf_hbm, o_ref,
                       kbuf, vbuf, lfbuf, sems, qbd_s, m_s, l_s, acc_s, suf_s, *, P, T, H, G, scale, n_steps, n_seq):
    d = LANES
    R = H // G
    HT = H * T
    step = pl.program_id(1)
    slot = _gather_pages(pt_ref, (k_hbm, v_hbm, lf_hbm), (kbuf, vbuf, lfbuf), sems, P=P, n_steps=n_steps,
                         n_seq=n_seq, chunk_of_step=lambda s: n_steps - 1 - s)

    def page_rows(buf, jj):
        ref = buf.at[slot, jj]
        return jnp.concatenate([ref[pl.ds(g, LANES, stride=G), :] for g in range(G)], axis=1).astype(BF16)

    def head_rows(x):
        return jnp.concatenate([jnp.broadcast_to(x[h:h + 1, :], (T, x.shape[1])) for h in range(H)], axis=0)

    @pl.when(step == 0)
    def _():
        q = q_ref[...] * scale
        zero = jnp.zeros((T, d), F32)
        blocks = []
        for h in range(H):
            g = h // R
            blocks.append(jnp.concatenate([q[:, h * d:(h + 1) * d] if gg == g else zero for gg in range(G)], axis=1))
        qbd = jnp.concatenate(blocks, axis=0).astype(BF16)
        qbd_s[...] = qbd
        pad = jnp.zeros((LANES - T, G * d), F32)
        kn = jnp.concatenate([kn_ref[...], pad], axis=0).astype(BF16)
        vn = jnp.concatenate([vn_ref[...], pad], axis=0).astype(BF16)
        cn = cn_ref[...]
        cn_rows = jnp.concatenate([cn[:, h:h + 1] for h in range(H)], axis=0)
        cn_cols = head_rows(cnt_ref[...])
        s = _dot_nt(qbd, kn) + cn_rows - cn_cols
        t_idx = lax.broadcasted_iota(jnp.int32, (HT, LANES), 0) % T
        s_idx = lax.broadcasted_iota(jnp.int32, (HT, LANES), 1)
        s = jnp.where(s_idx <= t_idx, s, NEG_INF)
        m = jnp.max(s, axis=-1, keepdims=True)
        p = jnp.exp(s - m)
        m_s[...] = m
        l_s[...] = jnp.sum(p, axis=-1, keepdims=True)
        acc_s[...] = _dot(p.astype(BF16), vn)
        suf_s[...] = cn_rows

    tri = (lax.broadcasted_iota(jnp.int32, (LANES, LANES), 0) <= lax.broadcasted_iota(jnp.int32, (LANES, LANES), 1))
    tri = jnp.where(tri, 1.0, 0.0).astype(BF16)
    qbd = qbd_s[...]
    suf = suf_s[...]
    cw_all = _split3_dot_right(lfbuf[slot].reshape(P * H, LANES), tri)
    s_parts = [None] * P
    for jj in range(P - 1, -1, -1):
        cw_rows = head_rows(cw_all[jj * H:(jj + 1) * H, :])
        tot = cw_rows[:, LANES - 1:LANES]
        s = _dot_nt(qbd, page_rows(kbuf, jj))
        s_parts[jj] = s + (suf + tot) - cw_rows
        suf = suf + tot
    suf_s[...] = suf
    m_old = m_s[...]
    m_new = m_old
    for jj in range(P):
        m_new = jnp.maximum(m_new, jnp.max(s_parts[jj], axis=-1, keepdims=True))
    alpha = jnp.exp(m_old - m_new)
    l_new = l_s[...] * alpha
    acc = acc_s[...] * alpha
    for jj in range(P):
        p = jnp.exp(s_parts[jj] - m_new)
        l_new = l_new + jnp.sum(p, axis=-1, keepdims=True)
        acc = acc + _dot(p.astype(BF16), page_rows(vbuf, jj))
    m_s[...] = m_new
    l_s[...] = l_new
    acc_s[...] = acc

    @pl.when(step == n_steps - 1)
    def _():
        out = acc / l_new
        o_ref[...] = jnp.concatenate(
            [out[h * T:(h + 1) * T, (h // R) * d:(h // R + 1) * d] for h in range(H)], axis=1).astype(o_ref.dtype)


def _split3_dot_right(x, mask_bf16):
    x1 = x.astype(BF16)
    r1 = x - x1.astype(F32)
    x2 = r1.astype(BF16)
    x3 = (r1 - x2.astype(F32)).astype(BF16)
    return _dot(x1, mask_bf16) + _dot(x2, mask_bf16) + _dot(x3, mask_bf16)


def fox_decode(z, q_cb, k_cb, v_cb, cnew, cnew_t, pool_k, pool_v, pool_lft, page_table, *, row0, T, H, G, name="fox_decode"):
    Bd, n_pages = page_table.shape
    P = FOX_PAGES_PER_STEP
    n_steps = n_pages // P
    d = LANES
    page = pool_lft.shape[2]
    assert page == LANES and pool_k.shape[1] == page * G and n_pages % P == 0 and row0 % T == 0
    r0 = row0 // T
    hbm = pl.BlockSpec(memory_space=pl.ANY)
    in_specs = [
        pl.BlockSpec((T, H * d), lambda b, s, pt: (r0 + b, q_cb)),
        pl.BlockSpec((T, G * d), lambda b, s, pt: (r0 + b, k_cb)),
        pl.BlockSpec((T, G * d), lambda b, s, pt: (r0 + b, v_cb)),
        pl.BlockSpec((T, LANES), lambda b, s, pt: (b, 0)),
        pl.BlockSpec((None, H, LANES), lambda b, s, pt: (b, 0, 0)),
        hbm, hbm, hbm,
    ]
    HT = H * T
    grid_spec = pltpu.PrefetchScalarGridSpec(
        num_scalar_prefetch=1,
        grid=(Bd, n_steps),
        in_specs=in_specs,
        out_specs=pl.BlockSpec((T, H * d), lambda b, s, pt: (b, 0)),
        scratch_shapes=[pltpu.VMEM((2, P, page * G, d), F32), pltpu.VMEM((2, P, page * G, d), F32),
                        pltpu.VMEM((2, P, H, page), F32), pltpu.SemaphoreType.DMA((3, 2)),
                        pltpu.VMEM((HT, G * d), BF16), pltpu.VMEM((HT, 1), F32), pltpu.VMEM((HT, 1), F32),
                        pltpu.VMEM((HT, G * d), F32), pltpu.VMEM((HT, 1), F32)],
    )
    return pl.pallas_call(
        functools.partial(_fox_decode_kernel, P=P, T=T, H=H, G=G, scale=d ** -0.5, n_steps=n_steps, n_seq=Bd),
        grid_spec=grid_spec,
        out_shape=jax.ShapeDtypeStruct((Bd * T, H * d), BF16),
        compiler_params=_params("arbitrary", "arbitrary"),
        name=name,
    )(page_table, z, z, z, cnew, cnew_t, pool_k, pool_v, pool_lft)


def _rope_block(v, c, s1, s2):
    return v * c + pltpu.roll(v, 96, 1) * s1 + pltpu.roll(v, 32, 1) * s2


def _mla_prep_kernel(z_ref, qg_ref, kg_ref, c_ref, s1_ref, s2_ref, qa_ref, ckv_ref, kpe_ref, *, rank):
    qa = z_ref[:, :rank]
    inv = lax.rsqrt(jnp.mean(qa * qa, axis=-1, keepdims=True) + EPS)
    qa_ref[...] = (qa * inv * qg_ref[...]).astype(qa_ref.dtype)
    ckv = z_ref[:, rank:2 * rank]
    inv = lax.rsqrt(jnp.mean(ckv * ckv, axis=-1, keepdims=True) + EPS)
    ckv_ref[...] = ckv * inv * kg_ref[...]
    kpe_ref[...] = _rope_block(z_ref[:, 2 * rank:2 * rank + LANES], c_ref[...], s1_ref[...], s2_ref[...])


def mla_prep(z1, q_norm, kv_norm, rope_tabs, *, rank, name="mla_prep"):
    M, W = z1.shape
    bm = _pick(M, (1024, 512, 256))
    row = lambda i: (i, 0)
    fix = lambda i: (0, 0)
    return pl.pallas_call(
        functools.partial(_mla_prep_kernel, rank=rank),
        grid=(M // bm,),
        in_specs=[pl.BlockSpec((bm, W), row), pl.BlockSpec((1, rank), fix), pl.BlockSpec((1, rank), fix)]
        + [pl.BlockSpec((bm, LANES), row)] * 3,
        out_specs=[pl.BlockSpec((bm, rank), row), pl.BlockSpec((bm, rank), row), pl.BlockSpec((bm, LANES), row)],
        out_shape=[jax.ShapeDtypeStruct((M, rank), BF16), jax.ShapeDtypeStruct((M, rank), F32),
                   jax.ShapeDtypeStruct((M, LANES), F32)],
        compiler_params=_params("parallel"),
        name=name,
    )(z1, q_norm.reshape(1, rank), kv_norm.reshape(1, rank), *rope_tabs)


def _qproj_kernel(a_ref, w_ref, c_ref, s1_ref, s2_ref, o_ref, *, heads_per_block, scale):
    acc = _dot(a_ref[...], w_ref[...].astype(BF16)) * scale
    c, s1, s2 = c_ref[...], s1_ref[...], s2_ref[...]
    for hh in range(heads_per_block):
        base = hh * 2 * LANES
        o_ref[:, base:base + LANES] = acc[:, base:base + LANES].astype(o_ref.dtype)
        o_ref[:, base + LANES:base + 2 * LANES] = _rope_block(
            acc[:, base + LANES:base + 2 * LANES], c, s1, s2).astype(o_ref.dtype)


def mla_qproj(qa_n, w_qb_pad, rope_tabs, *, scale, name="mla_qproj"):
    M, K = qa_n.shape
    N = w_qb_pad.shape[1]
    bm = _pick(M, (1024, 512, 256))
    bn = 1024
    row = lambda i, j: (i, 0)
    return pl.pallas_call(
        functools.partial(_qproj_kernel, heads_per_block=bn // (2 * LANES), scale=scale),
        grid=(M // bm, N // bn),
        in_specs=[pl.BlockSpec((bm, K), row), pl.BlockSpec((K, bn), lambda i, j: (0, j))]
        + [pl.BlockSpec((bm, LANES), row)] * 3,
        out_specs=pl.BlockSpec((bm, bn), lambda i, j: (i, j)),
        out_shape=jax.ShapeDtypeStruct((M, N), BF16),
        compiler_params=_params("parallel", "parallel"),
        name=name,
    )(qa_n, w_qb_pad, *rope_tabs)


def _qlat_kernel(q_ref, w_ref, o_ref, *, rank):
    q = q_ref[...]
    o_ref[0, :, :rank] = _dot(q[:, :LANES], w_ref[0].astype(BF16))
    o_ref[0, :, rank:] = q[:, LANES:].astype(F32)


def mla_qlat(q_full, w_ukt, *, row0, rows, rank, name="mla_qlat"):
    Hh = w_ukt.shape[0]
    r0 = row0 // rows
    assert row0 % rows == 0
    return pl.pallas_call(
        functools.partial(_qlat_kernel, rank=rank),
        grid=(Hh,),
        in_specs=[pl.BlockSpec((rows, 2 * LANES), lambda h: (r0, h)),
                  pl.BlockSpec((1, LANES, rank), lambda h: (h, 0, 0))],
        out_specs=pl.BlockSpec((1, rows, rank + LANES), lambda h: (h, 0, 0)),
        out_shape=jax.ShapeDtypeStruct((Hh, rows, rank + LANES), F32),
        compiler_params=_params("parallel"),
        name=name,
    )(q_full, w_ukt)


def _uv_kernel(x_ref, w_ref, o_ref):
    o_ref[...] = _dot(x_ref[0].astype(BF16), w_ref[...].astype(BF16)).astype(o_ref.dtype)


def mla_uv(o_lat, w_uv2d, name="mla_uv"):
    Hh, rows, rank = o_lat.shape
    return pl.pallas_call(
        _uv_kernel,
        grid=(Hh,),
        in_specs=[pl.BlockSpec((1, rows, rank), lambda h: (h, 0, 0)), pl.BlockSpec((rank, LANES), lambda h: (0, h))],
        out_specs=pl.BlockSpec((rows, LANES), lambda h: (0, h)),
        out_shape=jax.ShapeDtypeStruct((rows, Hh * LANES), BF16),
        compiler_params=_params("parallel"),
        name=name,
    )(o_lat, w_uv2d)


def _mla_decode_kernel(pt_ref, q_ref, cn_ref, kn_ref, c_hbm, r_hbm, o_ref, cbuf, rbuf, sems,
                       ql_s, qr_s, m_s, l_s, acc_s, *, P, T, Hh, rank, rope, n_steps, n_seq):
    HT = Hh * T
    step = pl.program_id(1)
    slot = _gather_pages(pt_ref, (c_hbm, r_hbm), (cbuf, rbuf), sems, P=P, n_steps=n_steps, n_seq=n_seq,
                         chunk_of_step=lambda s: s)

    @pl.when(step == 0)
    def _():
        q = q_ref[...].reshape(HT, rank + LANES)
        ql = q[:, :rank].astype(BF16)
        qr = q[:, rank:rank + rope].astype(BF16)
        ql_s[...] = ql
        qr_s[...] = qr
        cn = jnp.concatenate([cn_ref[...], jnp.zeros((LANES - T, rank), F32)], axis=0).astype(BF16)
        kn = jnp.concatenate([kn_ref[:, :rope], jnp.zeros((LANES - T, rope), F32)], axis=0).astype(BF16)
        s = _dot_nt(ql, cn) + _dot_nt(qr, kn)
        t_idx = lax.broadcasted_iota(jnp.int32, (HT, LANES), 0) % T
        s_idx = lax.broadcasted_iota(jnp.int32, (HT, LANES), 1)
        s = jnp.where(s_idx <= t_idx, s, NEG_INF)
        m = jnp.max(s, axis=-1, keepdims=True)
        p = jnp.exp(s - m)
        m_s[...] = m
        l_s[...] = jnp.sum(p, axis=-1, keepdims=True)
        acc_s[...] = _dot(p.astype(BF16), cn)

    ql = ql_s[...]
    qr = qr_s[...]
    n_pairs = P // 2
    ckv = [jnp.concatenate([cbuf[slot, 2 * jj].astype(BF16), cbuf[slot, 2 * jj + 1].astype(BF16)], axis=0)
           for jj in range(n_pairs)]
    s_parts = []
    for jj in range(n_pairs):
        kpe_t = jnp.concatenate([rbuf[slot, 2 * jj], rbuf[slot, 2 * jj + 1]], axis=1).astype(BF16)
        s_parts.append(_dot_nt(ql, ckv[jj]) + _dot(qr, kpe_t))
    m_old = m_s[...]
    m_new = m_old
    for jj in range(n_pairs):
        m_new = jnp.maximum(m_new, jnp.max(s_parts[jj], axis=-1, keepdims=True))
    alpha = jnp.exp(m_old - m_new)
    l_new = l_s[...] * alpha
    acc = acc_s[...] * alpha
    for jj in range(n_pairs):
        p = jnp.exp(s_parts[jj] - m_new)
        l_new = l_new + jnp.sum(p, axis=-1, keepdims=True)
        acc = acc + _dot(p.astype(BF16), ckv[jj])
    m_s[...] = m_new
    l_s[...] = l_new
    acc_s[...] = acc

    @pl.when(step == n_steps - 1)
    def _():
        o_ref[...] = (acc / l_new).reshape(Hh, T, rank)


def mla_decode(q_cat, ckv_n, kpe_r, pool_ckv, pool_kpe_t, page_table, *, row0, T, rope, name="mla_decode"):
    Hh = q_cat.shape[0]
    rank = pool_ckv.shape[-1]
    Bd, n_pages = page_table.shape
    P = MLA_PAGES_PER_STEP
    n_steps = n_pages // P
    page = pool_ckv.shape[1]
    assert page == LANES and pool_kpe_t.shape[1:] == (rope, page) and n_pages % P == 0 and P % 2 == 0 and row0 % T == 0
    r0 = row0 // T
    hbm = pl.BlockSpec(memory_space=pl.ANY)
    in_specs = [
        pl.BlockSpec((Hh, T, rank + LANES), lambda b, s, pt: (0, b, 0)),
        pl.BlockSpec((T, rank), lambda b, s, pt: (r0 + b, 0)),
        pl.BlockSpec((T, LANES), lambda b, s, pt: (r0 + b, 0)),
        hbm, hbm,
    ]
    HT = Hh * T
    grid_spec = pltpu.PrefetchScalarGridSpec(
        num_scalar_prefetch=1,
        grid=(Bd, n_steps),
        in_specs=in_specs,
        out_specs=pl.BlockSpec((Hh, T, rank), lambda b, s, pt: (0, b, 0)),
        scratch_shapes=[pltpu.VMEM((2, P, page, rank), F32), pltpu.VMEM((2, P, rope, page), F32),
                        pltpu.SemaphoreType.DMA((2, 2)),
                        pltpu.VMEM((HT, rank), BF16), pltpu.VMEM((HT, rope), BF16), pltpu.VMEM((HT, 1), F32),
                        pltpu.VMEM((HT, 1), F32), pltpu.VMEM((HT, rank), F32)],
    )
    return pl.pallas_call(
        functools.partial(_mla_decode_kernel, P=P, T=T, Hh=Hh, rank=rank, rope=rope, n_steps=n_steps, n_seq=Bd),
        grid_spec=grid_spec,
        out_shape=jax.ShapeDtypeStruct((Hh, Bd * T, rank), F32),
        compiler_params=_params("arbitrary", "arbitrary"),
        name=name,
    )(page_table, q_cat, ckv_n, kpe_r, pool_ckv, pool_kpe_t)


def _mem_attn_kernel(q_ref, k_ref, v_ref, o_ref, *, nb, r, Hm, scale, head_rows, mem_len):
    d = LANES
    pairs = [(n, h) for n in range(nb) for h in range(Hm)]

    def mem(ref, n, h):
        if head_rows:
            return ref[n, pl.ds(h, mem_len, stride=Hm), :].astype(BF16)
        return ref[n, :, h * d:(h + 1) * d].astype(BF16)

    q_all = (q_ref[...] * scale).astype(BF16)
    s_all = [_dot_nt(q_all[n * r:(n + 1) * r, h * d:(h + 1) * d], mem(k_ref, n, h)) for n, h in pairs]
    p_all, l_all = [], []
    for s in s_all:
        p = jnp.exp(s - jnp.max(s, axis=-1, keepdims=True))
        p_all.append(p.astype(BF16))
        l_all.append(jnp.sum(p, axis=-1, keepdims=True))
    o_all = [_dot(p, mem(v_ref, n, h)) / l for (n, h), p, l in zip(pairs, p_all, l_all)]
    o_ref[...] = jnp.concatenate(
        [jnp.concatenate(o_all[n * Hm:(n + 1) * Hm], axis=1) for n in range(nb)], axis=0).astype(o_ref.dtype)


def mem_attend(q, mk, mv, *, row0, rows_per_seq, nb, r, Hm, head_rows, layer=None, name="mem_attn"):
    n_seq, M, W = mk.shape[-3:]
    mem_len = M // Hm if head_rows else M
    blocks_per_seq = rows_per_seq // r
    assert (nb == 1 or blocks_per_seq == 1) and row0 % (nb * r) == 0 and n_seq % nb == 0
    r0 = row0 // (nb * r)
    n_blocks = n_seq * blocks_per_seq // nb
    if layer is None:
        kv_spec = pl.BlockSpec((nb, M, W), lambda i: (i // blocks_per_seq, 0, 0))
    else:
        kv_spec = pl.BlockSpec((None, nb, M, W), lambda i: (layer, i // blocks_per_seq, 0, 0))
    QW = Hm * LANES
    return pl.pallas_call(
        functools.partial(_mem_attn_kernel, nb=nb, r=r, Hm=Hm, scale=LANES ** -0.5, head_rows=head_rows,
                          mem_len=mem_len),
        grid=(n_blocks,),
        in_specs=[pl.BlockSpec((nb * r, QW), lambda i: (r0 + i, 0)), kv_spec, kv_spec],
        out_specs=pl.BlockSpec((nb * r, QW), lambda i: (i, 0)),
        out_shape=jax.ShapeDtypeStruct((n_seq * rows_per_seq, QW), BF16),
        compiler_params=_params("parallel"),
        name=name,
    )(q, mk, mv)


def _pad_cols(w, n):
    return jnp.pad(w, ((0, 0), (0, n - w.shape[1])))


def kernel(x_prompt, x_sample, mem_prompt, page_table, cache_fox_k, cache_fox_v, cache_fox_lf, state_hgrn, cache_mla_ckv, cache_mla_kpe, cache_mem_k, cache_mem_v, ev_norm, ev_w_in, ev_fox_bf, ev_hg_gamma, ev_hg_norm, ev_w_out, od_norm, od_w_in, od_q_norm, od_w_qb, od_kv_norm, od_w_uk, od_w_uv, od_w_out, xa_norm, xa_mem_norm, xa_wq, xa_wk, xa_wv, xa_wo, ff_norm, ff_w1, ff_w2, final_norm):
    B, S, D = x_prompt.shape
    Bd, T, _ = x_sample.shape
    depth = ff_w1.shape[0]
    n_pages = page_table.shape[1]
    page = cache_fox_k.shape[2]
    past_len = n_pages * page
    NP, NS = B * S, Bd * T
    N = NP + NS
    G = cache_fox_k.shape[3]
    d = cache_fox_k.shape[4]
    H = cache_fox_lf.shape[3]
    HGH = state_hgrn.shape[2]
    assert d == LANES and H * d == HGH * LANES
    W8 = H * d
    rank = cache_mla_ckv.shape[-1]
    rope = cache_mla_kpe.shape[-1]
    Hm = od_w_uk.shape[2]
    nope = od_w_uk.shape[3]
    mem_len = mem_prompt.shape[1]
    Hx = cache_mem_k.shape[3]
    XW = Hx * cache_mem_k.shape[4]
    assert nope == LANES and od_w_uv.shape[3] == LANES and rope * 2 == LANES

    x = jnp.concatenate([x_prompt.reshape(NP, D), x_sample.reshape(NS, D)], axis=0)

    half = rope // 2
    pos = jnp.concatenate([jnp.tile(jnp.arange(S), B), jnp.tile(past_len + jnp.arange(T), Bd)]).astype(F32)
    inv_freq = ROPE_THETA ** (-jnp.arange(half, dtype=F32) / half)
    ang = pos[:, None] * inv_freq[None, :]
    cos, sin = jnp.cos(ang), jnp.sin(ang)
    zero = jnp.zeros_like(cos)
    rope_tabs = (jnp.concatenate([cos, cos, zero, zero], axis=1),
                 jnp.concatenate([-sin, zero, zero, zero], axis=1),
                 jnp.concatenate([zero, sin, zero, zero], axis=1))

    outs = {k: [] for k in ("p_fk", "p_fv", "p_flf", "p_hs", "p_ckv", "p_kpe", "p_mk", "p_mv",
                            "s_fk", "s_fv", "s_flf", "s_hs", "s_ckv", "s_kpe")}
    for layer in range(depth):
        if layer % 2 == 0:
            e = layer // 2
            w = ev_w_in[e]
            o_fq, o_fk, o_fv, o_fz, o_hq = 0, W8, W8 + G * d, W8 + 2 * G * d, W8 + 2 * G * d + H
            w_re = jnp.concatenate([w[:, o_hq:], w[:, o_fq:o_fz], _pad_cols(w[:, o_fz:o_hq], LANES)], axis=1)
            c_fq, c_fk, c_fv, c_fz = 4 * W8, 5 * W8, 5 * W8 + G * d, 5 * W8 + 2 * G * d
            n_cols = c_fz + LANES
            n_pad = -(-n_cols // 768) * 768
            z = norm_mm(x, ev_norm[e], _pad_cols(w_re, n_pad), name="even_in")
            bias = jnp.pad(ev_fox_bf[e], (0, LANES - H))
            zf = z[:, c_fz:c_fz + LANES]
            lf_p, cum_p = fox_gates(zf[:NP], bias, S, name="fox_gates_prompt")
            lf_s, cum_s = fox_gates(zf[NP:], bias, T, name="fox_gates_sample")
            lb = jnp.cumsum(jax.nn.softmax(ev_hg_gamma.astype(F32), axis=0), axis=0)[e]
            cq = cum_p[:, :H].reshape(B, S, H).transpose(0, 2, 1).reshape(B * H, 1, S)
            fo_p = flash_causal(z, lambda h: c_fq // d + h, d,
                                [(z, lambda g: c_fk // d + g, d)], z, lambda g: c_fv // d + g, d,
                                B=B, S=S, H=H, R=H // G, scale=d ** -0.5, cum=(cq, cum_p), name="fox_prompt")
            hg_p, hs_p = hgrn2(z, (0, 1, 2, 3), lb, ev_hg_norm[e], None, row0=0, n_seq=B, T=S, nb=1, H=HGH,
                               name="hgrn_prompt")
            lft = cache_fox_lf[e].transpose(0, 2, 1)
            cum_st = jnp.pad(cum_s[:, :H].reshape(Bd, T, H).transpose(0, 2, 1), ((0, 0), (0, 0), (0, LANES - T)))
            fo_s = fox_decode(z, c_fq // W8, c_fk // (G * d), c_fv // (G * d), cum_s, cum_st,
                              cache_fox_k[e].reshape(-1, page * G, d), cache_fox_v[e].reshape(-1, page * G, d),
                              lft, page_table, row0=NP, T=T, H=H, G=G)
            hg_s, hs_s = hgrn2(z, (0, 1, 2, 3), lb, ev_hg_norm[e], state_hgrn[e], row0=NP, n_seq=Bd, T=T, nb=2,
                               H=HGH, name="hgrn_sample")
            cat = jnp.concatenate([jnp.concatenate([fo_p, fo_s], axis=0), jnp.concatenate([hg_p, hg_s], axis=0)], axis=1)
            x = mm(cat, ev_w_out, x, layer=e, name="even_out")
            fk = z[:, c_fk:c_fk + G * d]
            fv = z[:, c_fv:c_fv + G * d]
            outs["p_fk"].append(fk[:NP].reshape(B, S, G, d))
            outs["p_fv"].append(fv[:NP].reshape(B, S, G, d))
            outs["p_flf"].append(lf_p[:, :H].reshape(B, S, H))
            outs["p_hs"].append(hs_p)
            outs["s_fk"].append(fk[NP:].reshape(Bd, T, G, d))
            outs["s_fv"].append(fv[NP:].reshape(Bd, T, G, d))
            outs["s_flf"].append(lf_s[:, :H].reshape(Bd, T, H))
            outs["s_hs"].append(hs_s)
        else:
            o = layer // 2
            scale = (nope + rope) ** -0.5
            n_in = 2 * rank + LANES
            z1 = norm_mm(x, od_norm[o], _pad_cols(od_w_in[o], n_in), name="mla_in")
            qa_n, ckv_n, kpe_r = mla_prep(z1, od_q_norm[o], od_kv_norm[o], rope_tabs, rank=rank)
            wq = od_w_qb[o].reshape(rank, Hm, nope + rope)
            wq = jnp.pad(wq, ((0, 0), (0, 0), (0, 2 * LANES - nope - rope))).reshape(rank, Hm * 2 * LANES)
            q_full = mla_qproj(qa_n, wq, rope_tabs, scale=scale)
            w_kv = jnp.concatenate([od_w_uk[o].reshape(rank, Hm * nope), od_w_uv[o].reshape(rank, Hm * LANES)], axis=1)
            kv = mm(ckv_n[:NP], w_kv, out_dtype=BF16, name="mla_kv_up")
            at_p = flash_causal(q_full, lambda h: h, 2 * LANES,
                                [(kv, lambda g: g, LANES), (kpe_r, lambda g: 0, LANES)], kv, lambda g: Hm + g, LANES,
                                B=B, S=S, H=Hm, R=1, scale=1.0, name="mla_prompt")
            q_cat = mla_qlat(q_full, od_w_uk[o].transpose(1, 2, 0), row0=NP, rows=NS, rank=rank)
            o_lat = mla_decode(q_cat, ckv_n, kpe_r, cache_mla_ckv[o], cache_mla_kpe[o].transpose(0, 2, 1), page_table,
                               row0=NP, T=T, rope=rope)
            at_s = mla_uv(o_lat, od_w_uv[o].reshape(rank, Hm * LANES))
            x = mm(jnp.concatenate([at_p, at_s], axis=0), od_w_out, x, layer=o, name="mla_out")
            outs["p_ckv"].append(ckv_n[:NP].reshape(B, S, rank))
            outs["p_kpe"].append(kpe_r[:NP, :rope].reshape(B, S, rope))
            outs["s_ckv"].append(ckv_n[NP:].reshape(Bd, T, rank))
            outs["s_kpe"].append(kpe_r[NP:, :rope].reshape(Bd, T, rope))
        w_mkv = jnp.concatenate([xa_wk[layer], xa_wv[layer]], axis=1)
        mkv = norm_mm(mem_prompt.reshape(B * mem_len, D), xa_mem_norm[layer], w_mkv, name="mem_kv")
        mk, mv = mkv[:, :XW], mkv[:, XW:]
        outs["p_mk"].append(mk.reshape(B, mem_len, Hx, XW // Hx))
        outs["p_mv"].append(mv.reshape(B, mem_len, Hx, XW // Hx))
        xq = norm_mm(x, xa_norm[layer], xa_wq, layer=layer, name="mem_q")
        xo_p = mem_attend(xq, mk.reshape(B, mem_len, XW), mv.reshape(B, mem_len, XW), row0=0, rows_per_seq=S,
                          nb=1, r=512, Hm=Hx, head_rows=False, name="mem_attn_prompt")
        xo_s = mem_attend(xq, cache_mem_k.reshape(depth, Bd, mem_len * Hx, XW // Hx),
                          cache_mem_v.reshape(depth, Bd, mem_len * Hx, XW // Hx), layer=layer,
                          row0=NP, rows_per_seq=T, nb=8, r=T, Hm=Hx, head_rows=True, name="mem_attn_sample")
        x = mm(jnp.concatenate([xo_p, xo_s], axis=0), xa_wo, x, layer=layer, name="mem_out")
        hmid = norm_mm(x, ff_norm[layer], ff_w1, layer=layer, act="relu2", out_dtype=BF16, name="ff_up")
        x = mm(hmid, ff_w2, x, layer=layer, name="ff_down")
    y_p = rmsnorm_rows(x, final_norm, row0=0, rows=NP, name="final_norm_prompt")
    y_s = rmsnorm_rows(x, final_norm, row0=NP, rows=NS, name="final_norm_sample")
    st = lambda k: jnp.stack(outs[k])
    return (y_p.reshape(B, S, D), y_s.reshape(Bd, T, D),
            st("p_fk"), st("p_fv"), st("p_flf"), st("p_hs"), st("p_ckv"), st("p_kpe"), st("p_mk"), st("p_mv"),
            st("s_fk"), st("s_fv"), st("s_flf"), st("s_hs"), st("s_ckv"), st("s_kpe"))
```
